```python
import jax, jax.numpy as jnp
from jax import lax
import numpy as np

D_MODEL = 1024
BATCH = 16
SEQ = 2048
DEPTH = 1

N_HEADS = 8
N_KV_HEADS = 2
HEAD_DIM = 128
Q_BLOCK = 128
GRID_W = 64
ROPE_THETA = 10000.0
AXIS_ROPE_DIM = HEAD_DIM // 2
D_CONV = D_MODEL
CONV_W = 3
D_FF = 2816
EPS = 1e-6
Q_W = N_HEADS * HEAD_DIM
KV_W = N_KV_HEADS * HEAD_DIM
IN_COLS = Q_W + 2 * KV_W + 3 * D_CONV + 2 * D_MODEL

kernel_name = "hybrid_gqa_shortconv_convffn_block"


def rmsnorm(x, g):
    xf = x.astype(jnp.float32)
    y = xf * lax.rsqrt(jnp.mean(xf * xf, axis=-1, keepdims=True) + EPS)
    return (y * g.astype(jnp.float32)).astype(x.dtype)


def dwconv3(x, w):
    xp = jnp.pad(x, ((0, 0), (1, 1), (0, 0)))
    return xp[:, :-2] * w[0] + xp[:, 1:-1] * w[1] + xp[:, 2:] * w[2]


def rope_tables(pos):
    half = AXIS_ROPE_DIM // 2
    freqs = ROPE_THETA ** (-jnp.arange(half, dtype=jnp.float32) / half)
    ang = pos[:, None] * freqs[None, :]
    ang = jnp.concatenate([ang, ang], axis=-1)
    return jnp.cos(ang), jnp.sin(ang)


def apply_rot(x, cos, sin):
    d = x.shape[-1] // 2
    rot = jnp.concatenate([-x[..., d:], x[..., :d]], axis=-1)
    c = cos[:, None, :].astype(x.dtype)
    s = sin[:, None, :].astype(x.dtype)
    return x * c + rot * s


def rope2d(x, tabs):
    cos_r, sin_r, cos_c, sin_c = tabs
    xr = apply_rot(x[..., :AXIS_ROPE_DIM], cos_r, sin_r)
    xc = apply_rot(x[..., AXIS_ROPE_DIM:], cos_c, sin_c)
    return jnp.concatenate([xr, xc], axis=-1)


def blocked_gqa(q, k, v):
    B, S = q.shape[0], q.shape[1]
    nblk = S // Q_BLOCK
    G = N_HEADS // N_KV_HEADS
    scale = HEAD_DIM ** -0.5
    qb = q.reshape(B, nblk, Q_BLOCK, N_KV_HEADS, G, HEAD_DIM).transpose(1, 0, 3, 4, 2, 5)
    kt = k.transpose(0, 2, 1, 3)
    vt = v.transpose(0, 2, 1, 3)

    def one_block(qblk):
        s = jnp.einsum('bkgqd,bksd->bkgqs', qblk, kt, preferred_element_type=jnp.float32) * scale
        p = jax.nn.softmax(s, axis=-1)
        return jnp.einsum('bkgqs,bksd->bkgqd', p.astype(vt.dtype), vt)

    o = lax.map(one_block, qb)
    return o.transpose(1, 0, 4, 2, 3, 5).reshape(B, S, N_HEADS * HEAD_DIM)


def mixer_sublayer(x, tabs, pre_g, w_in, gate_b, q_norm_g, k_norm_g, conv_w,
                   w_attn_proj, w_conv_proj, w_out, post_g):
    B, S, _ = x.shape
    h = rmsnorm(x, pre_g)
    z = h @ w_in
    splits = np.cumsum([Q_W, KV_W, KV_W, D_CONV, D_CONV, D_CONV, D_MODEL]).tolist()
    q, k, v, u, b_gate, c_gate, ga, gb = jnp.split(z, splits, axis=-1)
    q = rmsnorm(q.reshape(B, S, N_HEADS, HEAD_DIM), q_norm_g)
    k = rmsnorm(k.reshape(B, S, N_KV_HEADS, HEAD_DIM), k_norm_g)
    v = v.reshape(B, S, N_KV_HEADS, HEAD_DIM)
    q = rope2d(q, tabs)
    k = rope2d(k, tabs)
    y_a = blocked_gqa(q, k, v) @ w_attn_proj
    y_b = (b_gate * dwconv3(c_gate * u, conv_w)) @ w_conv_proj
    g_a = jax.nn.sigmoid(ga + gate_b[:D_MODEL])
    g_b = jax.nn.sigmoid(gb + gate_b[D_MODEL:])
    out = (g_a * y_a + g_b * y_b) @ w_out
    return rmsnorm(out, post_g)


def ffn_sublayer(x, pre_g, w_up, conv_w, w_down, post_g):
    h = rmsnorm(x, pre_g)
    up = h @ w_up
    a, b = jnp.split(up, 2, axis=-1)
    hidden = jax.nn.gelu(dwconv3(a, conv_w), approximate=True) * b
    return rmsnorm(hidden @ w_down, post_g)


def setup_inputs(seed: int = 0) -> dict:
    key = jax.random.key(seed)
    ks = jax.random.split(key, 16)
    nrm = lambda k, shape, s: jax.random.normal(k, shape, jnp.float32) * s
    gain = lambda k, n: 1.0 + nrm(k, (DEPTH, n), 0.02)
    return {
        "x": nrm(ks[0], (BATCH, SEQ, D_MODEL), 1.0),
        "mix_pre_g": gain(ks[1], D_MODEL),
        "w_in": nrm(ks[2], (DEPTH, D_MODEL, IN_COLS), D_MODEL ** -0.5),
        "gate_b": nrm(ks[3], (DEPTH, 2 * D_MODEL), 0.01),
        "q_norm_g": gain(ks[4], HEAD_DIM),
        "k_norm_g": gain(ks[5], HEAD_DIM),
        "mix_conv_w": nrm(ks[6], (DEPTH, CONV_W, D_CONV), CONV_W ** -0.5),
        "w_attn_proj": nrm(ks[7], (DEPTH, Q_W, D_MODEL), Q_W ** -0.5),
        "w_conv_proj": nrm(ks[8], (DEPTH, D_CONV, D_MODEL), D_CONV ** -0.5),
        "w_out": nrm(ks[9], (DEPTH, D_MODEL, D_MODEL), D_MODEL ** -0.5),
        "mix_post_g": gain(ks[10], D_MODEL),
        "ffn_pre_g": gain(ks[11], D_MODEL),
        "w_up": nrm(ks[12], (DEPTH, D_MODEL, 2 * D_FF), D_MODEL ** -0.5),
        "ffn_conv_w": nrm(ks[13], (DEPTH, CONV_W, D_FF), CONV_W ** -0.5),
        "w_down": nrm(ks[14], (DEPTH, D_FF, D_MODEL), D_FF ** -0.5),
        "ffn_post_g": gain(ks[15], D_MODEL),
    }


def reference(x, mix_pre_g, w_in, gate_b, q_norm_g, k_norm_g, mix_conv_w,
              w_attn_proj, w_conv_proj, w_out, mix_post_g, ffn_pre_g, w_up,
              ffn_conv_w, w_down, ffn_post_g):
    S = x.shape[1]
    ROWS = S // GRID_W
    rows, cols = jnp.meshgrid(jnp.arange(ROWS), jnp.arange(GRID_W), indexing='ij')
    row_pos = rows.reshape(-1).astype(jnp.float32)
    col_pos = cols.reshape(-1).astype(jnp.float32)
    cos_r, sin_r = rope_tables(row_pos)
    cos_c, sin_c = rope_tables(col_pos)
    tabs = (cos_r, sin_r, cos_c, sin_c)
    for l in range(DEPTH):
        x = x + mixer_sublayer(x, tabs, mix_pre_g[l], w_in[l], gate_b[l], q_norm_g[l],
                               k_norm_g[l], mix_conv_w[l], w_attn_proj[l], w_conv_proj[l],
                               w_out[l], mix_post_g[l])
        x = x + ffn_sublayer(x, ffn_pre_g[l], w_up[l], ffn_conv_w[l], w_down[l], ffn_post_g[l])
    return x
```

```python
import functools

import jax
import jax.numpy as jnp
from jax import lax
from jax.experimental import pallas as pl
from jax.experimental.pallas import tpu as pltpu

N_HEADS = 8
N_KV_HEADS = 2
HEAD_DIM = 128
GROUP = N_HEADS // N_KV_HEADS
GRID_W = 64
ROPE_THETA = 10000.0
AXIS_ROPE_DIM = HEAD_DIM // 2
EPS = 1e-6
SCALE = HEAD_DIM ** -0.5

HALO = 16
V7X_VMEM_LIMIT_BYTES = 56 * 1024 * 1024

ROW_TILE = 512
PROJ_ROWS = 256
Q_ROWS = 256
FF_CHUNK = 256

F32 = jnp.float32
BF16 = jnp.bfloat16


def _rms_scale(v):
    return lax.rsqrt(jnp.mean(v * v, axis=-1, keepdims=True) + EPS)


def _sigmoid(v):
    return 1.0 / (1.0 + jnp.exp(-v))


def _gelu_tanh(v):
    c = 0.7978845608028654
    return 0.5 * v * (1.0 + jnp.tanh(c * (v + 0.044715 * (v * v * v))))


def _attention_kernel(x_ref, g_ref, w_ref, qg_ref, kg_ref, cos_ref, sin_ref, o_ref,
                      q_scr, k_scr, v_scr):
    seq = x_ref.shape[1]
    q_w = N_HEADS * HEAD_DIM
    kv_w = N_KV_HEADS * HEAD_DIM
    lane = lax.broadcasted_iota(jnp.int32, (PROJ_ROWS, HEAD_DIM), 1)
    first_half = (lane & (AXIS_ROPE_DIM // 2)) == 0

    def norm_rope(z, gain, cos, sin):
        zn = z * _rms_scale(z) * gain
        rot = jnp.where(first_half,
                        pltpu.roll(zn, HEAD_DIM - AXIS_ROPE_DIM // 2, 1),
                        pltpu.roll(zn, AXIS_ROPE_DIM // 2, 1))
        return zn * cos + rot * sin

    def proj_step(r, carry):
        rows = pl.ds(pl.multiple_of(r * PROJ_ROWS, PROJ_ROWS), PROJ_ROWS)
        xt = x_ref[0, rows, :]
        h = (xt * _rms_scale(xt) * g_ref[...]).astype(BF16)
        z = jnp.dot(h, w_ref[...], preferred_element_type=F32)
        cos = cos_ref[rows, :]
        sin = sin_ref[rows, :]
        for hh in range(N_HEADS):
            zq = z[:, hh * HEAD_DIM:(hh + 1) * HEAD_DIM]
            q_scr[hh, rows, :] = (norm_rope(zq, qg_ref[...], cos, sin) * SCALE).astype(BF16)
        for kh in range(N_KV_HEADS):
            zk = z[:, q_w + kh * HEAD_DIM:q_w + (kh + 1) * HEAD_DIM]
            k_scr[kh, rows, :] = norm_rope(zk, kg_ref[...], cos, sin).astype(BF16)
            zv = z[:, q_w + kv_w + kh * HEAD_DIM:q_w + kv_w + (kh + 1) * HEAD_DIM]
            v_scr[kh, rows, :] = zv.astype(BF16)
        return carry

    lax.fori_loop(0, seq // PROJ_ROWS, proj_step, 0)

    for hh in range(N_HEADS):
        kh = hh // GROUP

        def attn_step(t, carry, hh=hh, kh=kh):
            rows = pl.ds(pl.multiple_of(t * Q_ROWS, Q_ROWS), Q_ROWS)
            q = q_scr[hh, rows, :]
            s = lax.dot_general(q, k_scr[kh], (((1,), (1,)), ((), ())),
                                preferred_element_type=F32)
            m = jnp.max(s, axis=-1, keepdims=True)
            p = jnp.exp(s - m)
            denom = jnp.sum(p, axis=-1, keepdims=True)
            o = jnp.dot(p.astype(BF16), v_scr[kh], preferred_element_type=F32)
            o_ref[0, rows, hh * HEAD_DIM:(hh + 1) * HEAD_DIM] = (o * (1.0 / denom)).astype(BF16)
            return carry

        lax.fori_loop(0, seq // Q_ROWS, attn_step, 0)


def _attention(x, pre_g, w_qkv, q_g, k_g, cos_t, sin_t):
    b, s, d = x.shape
    q_w = N_HEADS * HEAD_DIM
    const2 = lambda i: (0, 0)
    return pl.pallas_call(
        _attention_kernel,
        out_shape=jax.ShapeDtypeStruct((b, s, q_w), BF16),
        grid=(b,),
        in_specs=[
            pl.BlockSpec((1, s, d), lambda i: (i, 0, 0)),
            pl.BlockSpec((1, d), const2),
            pl.BlockSpec(w_qkv.shape, const2),
            pl.BlockSpec((1, HEAD_DIM), const2),
            pl.BlockSpec((1, HEAD_DIM), const2),
            pl.BlockSpec((s, HEAD_DIM), const2),
            pl.BlockSpec((s, HEAD_DIM), const2),
        ],
        out_specs=pl.BlockSpec((1, s, q_w), lambda i: (i, 0, 0)),
        scratch_shapes=[
            pltpu.VMEM((N_HEADS, s, HEAD_DIM), BF16),
            pltpu.VMEM((N_KV_HEADS, s, HEAD_DIM), BF16),
            pltpu.VMEM((N_KV_HEADS, s, HEAD_DIM), BF16),
        ],
        compiler_params=pltpu.CompilerParams(
            dimension_semantics=("arbitrary",), vmem_limit_bytes=V7X_VMEM_LIMIT_BYTES),
        name="attention",
    )(x, pre_g, w_qkv, q_g, k_g, cos_t, sin_t)


def _halo_specs(seq, d):
    per_tile = ROW_TILE // HALO
    last = seq // HALO - 1
    main = pl.BlockSpec((1, ROW_TILE, d), lambda b, i: (b, i, 0))
    prev = pl.BlockSpec((1, HALO, d), lambda b, i: (b, jnp.maximum(i * per_tile - 1, 0), 0))
    nxt = pl.BlockSpec((1, HALO, d), lambda b, i: (b, jnp.minimum((i + 1) * per_tile, last), 0))
    return main, prev, nxt


def _normed_with_halo(h_scr, x_ref, xp_ref, xn_ref, g_ref):
    i = pl.program_id(1)
    gain = g_ref[...]

    def normed(v):
        return v * _rms_scale(v) * gain

    keep_prev = (i > 0).astype(F32)
    keep_next = (i < pl.num_programs(1) - 1).astype(F32)
    h_scr[0:HALO, :] = (normed(xp_ref[0]) * keep_prev).astype(BF16)
    h_scr[HALO:HALO + ROW_TILE, :] = normed(x_ref[0]).astype(BF16)
    h_scr[HALO + ROW_TILE:, :] = (normed(xn_ref[0]) * keep_next).astype(BF16)


def _conv3(src_ref, w):
    prev = src_ref[pl.ds(HALO - 1, ROW_TILE), :]
    mid = src_ref[pl.ds(HALO, ROW_TILE), :]
    nxt = src_ref[pl.ds(HALO + 1, ROW_TILE), :]
    return prev * w[0:1, :] + mid * w[1:2, :] + nxt * w[2:3, :]


def _mixer_kernel(x_ref, xp_ref, xn_ref, o_ref, pre_g_ref, w_u_ref, w_c_ref, w_b_ref,
                  w_ga_ref, w_gb_ref, gate_b_ref, conv_w_ref, w_attn_ref, w_conv_ref,
                  w_out_ref, post_g_ref, out_ref, h_scr, cu_scr):
    d = x_ref.shape[2]
    _normed_with_halo(h_scr, x_ref, xp_ref, xn_ref, pre_g_ref)
    h_ext = h_scr[...]
    u = jnp.dot(h_ext, w_u_ref[...], preferred_element_type=F32)
    c = jnp.dot(h_ext, w_c_ref[...], preferred_element_type=F32)
    cu_scr[...] = c * u
    h = h_scr[pl.ds(HALO, ROW_TILE), :]
    b_gate = jnp.dot(h, w_b_ref[...], preferred_element_type=F32)
    conv = (b_gate * _conv3(cu_scr, conv_w_ref[...])).astype(BF16)
    y_b = jnp.dot(conv, w_conv_ref[...], preferred_element_type=F32)
    y_a = jnp.dot(o_ref[0], w_attn_ref[...], preferred_element_type=F32)
    g_a = _sigmoid(jnp.dot(h, w_ga_ref[...], preferred_element_type=F32) + gate_b_ref[:, :d])
    g_b = _sigmoid(jnp.dot(h, w_gb_ref[...], preferred_element_type=F32) + gate_b_ref[:, d:])
    merged = (g_a * y_a + g_b * y_b).astype(BF16)
    out = jnp.dot(merged, w_out_ref[...], preferred_element_type=F32)
    out_ref[0] = x_ref[0] + out * _rms_scale(out) * post_g_ref[...]


def _mixer(x, o, pre_g, w_u, w_c, w_b, w_ga, w_gb, gate_b, conv_w, w_attn, w_conv, w_out, post_g):
    b, s, d = x.shape
    main, prev, nxt = _halo_specs(s, d)
    const = lambda bi, i: (0, 0)
    full = lambda a: pl.BlockSpec(a.shape, const)
    return pl.pallas_call(
        _mixer_kernel,
        out_shape=jax.ShapeDtypeStruct((b, s, d), F32),
        grid=(b, s // ROW_TILE),
        in_specs=[main, prev, nxt,
                  pl.BlockSpec((1, ROW_TILE, o.shape[2]), lambda bi, i: (bi, i, 0)),
                  full(pre_g), full(w_u), full(w_c), full(w_b), full(w_ga), full(w_gb),
                  full(gate_b), full(conv_w), full(w_attn), full(w_conv), full(w_out),
                  full(post_g)],
        out_specs=pl.BlockSpec((1, ROW_TILE, d), lambda bi, i: (bi, i, 0)),
        scratch_shapes=[
            pltpu.VMEM((ROW_TILE + 2 * HALO, d), BF16),
            pltpu.VMEM((ROW_TILE + 2 * HALO, w_u.shape[1]), F32),
        ],
        compiler_params=pltpu.CompilerParams(
            dimension_semantics=("arbitrary", "arbitrary"),
            vmem_limit_bytes=V7X_VMEM_LIMIT_BYTES),
        name="mixer",
    )(x, x, x, o, pre_g, w_u, w_c, w_b, w_ga, w_gb, gate_b, conv_w, w_attn, w_conv, w_out, post_g)


def _ffn_kernel(x_ref, xp_ref, xn_ref, pre_g_ref, w_a_ref, w_b_ref, conv_w_ref, w_d_ref,
                post_g_ref, out_ref, h_scr, a_scr, acc_scr):
    _normed_with_halo(h_scr, x_ref, xp_ref, xn_ref, pre_g_ref)
    acc_scr[...] = jnp.zeros_like(acc_scr)

    def chunk(j, carry):
        a_scr[...] = jnp.dot(h_scr[...], w_a_ref[j], preferred_element_type=F32)
        gate = _gelu_tanh(_conv3(a_scr, conv_w_ref[j]))
        val = jnp.dot(h_scr[pl.ds(HALO, ROW_TILE), :], w_b_ref[j], preferred_element_type=F32)
        hidden = (gate * val).astype(BF16)
        acc_scr[...] += jnp.dot(hidden, w_d_ref[j], preferred_element_type=F32)
        return carry

    lax.fori_loop(0, w_a_ref.shape[0], chunk, 0)
    out = acc_scr[...]
    out_ref[0] = x_ref[0] + out * _rms_scale(out) * post_g_ref[...]


def _ffn(x, pre_g, w_a, w_b, conv_w, w_d, post_g):
    b, s, d = x.shape
    main, prev, nxt = _halo_specs(s, d)
    const2 = lambda bi, i: (0, 0)
    const3 = lambda bi, i: (0, 0, 0)
    return pl.pallas_call(
        _ffn_kernel,
        out_shape=jax.ShapeDtypeStruct((b, s, d), F32),
        grid=(b, s // ROW_TILE),
        in_specs=[main, prev, nxt,
                  pl.BlockSpec(pre_g.shape, const2),
                  pl.BlockSpec(w_a.shape, const3),
                  pl.BlockSpec(w_b.shape, const3),
                  pl.BlockSpec(conv_w.shape, const3),
                  pl.BlockSpec(w_d.shape, const3),
                  pl.BlockSpec(post_g.shape, const2)],
        out_specs=pl.BlockSpec((1, ROW_TILE, d), lambda bi, i: (bi, i, 0)),
        scratch_shapes=[
            pltpu.VMEM((ROW_TILE + 2 * HALO, d), BF16),
            pltpu.VMEM((ROW_TILE + 2 * HALO, FF_CHUNK), F32),
            pltpu.VMEM((ROW_TILE, d), F32),
        ],
        compiler_params=pltpu.CompilerParams(
            dimension_semantics=("arbitrary", "arbitrary"),
            vmem_limit_bytes=V7X_VMEM_LIMIT_BYTES),
        name="ffn",
    )(x, x, x, pre_g, w_a, w_b, conv_w, w_d, post_g)


def _rope_tables(seq):
    half = AXIS_ROPE_DIM // 2
    t = jnp.arange(seq)
    freqs = ROPE_THETA ** (-jnp.arange(half, dtype=F32) / half)

    def axis_tables(pos):
        ang = pos.astype(F32)[:, None] * freqs[None, :]
        ang = jnp.concatenate([ang, ang], axis=-1)
        return jnp.cos(ang), jnp.sin(ang)

    cos_r, sin_r = axis_tables(t // GRID_W)
    cos_c, sin_c = axis_tables(t % GRID_W)
    sign = jnp.where(jnp.arange(AXIS_ROPE_DIM) < half, -1.0, 1.0).astype(F32)
    cos_t = jnp.concatenate([cos_r, cos_c], axis=-1)
    sin_t = jnp.concatenate([sin_r * sign, sin_c * sign], axis=-1)
    return cos_t, sin_t


def _chunk_cols(w, n):
    return w.reshape(w.shape[0], n, FF_CHUNK).transpose(1, 0, 2)


def kernel(x, mix_pre_g, w_in, gate_b, q_norm_g, k_norm_g, mix_conv_w, w_attn_proj, w_conv_proj,
           w_out, mix_post_g, ffn_pre_g, w_up, ffn_conv_w, w_down, ffn_post_g):
    depth = w_in.shape[0]
    seq, d = x.shape[1], x.shape[2]
    q_w = N_HEADS * HEAD_DIM
    kv_w = N_KV_HEADS * HEAD_DIM
    d_conv = mix_conv_w.shape[2]
    d_ff = ffn_conv_w.shape[2]
    n_ff = d_ff // FF_CHUNK
    cos_t, sin_t = _rope_tables(seq)
    for l in range(depth):
        w = w_in[l].astype(BF16)
        o0 = q_w + 2 * kv_w
        w_qkv = w[:, :o0]
        w_u = w[:, o0:o0 + d_conv]
        w_b = w[:, o0 + d_conv:o0 + 2 * d_conv]
        w_c = w[:, o0 + 2 * d_conv:o0 + 3 * d_conv]
        w_ga = w[:, o0 + 3 * d_conv:o0 + 3 * d_conv + d]
        w_gb = w[:, o0 + 3 * d_conv + d:]
        attn = _attention(x, mix_pre_g[l][None], w_qkv, q_norm_g[l][None], k_norm_g[l][None],
                          cos_t, sin_t)
        x = _mixer(x, attn, mix_pre_g[l][None], w_u, w_c, w_b, w_ga, w_gb, gate_b[l][None],
                   mix_conv_w[l], w_attn_proj[l].astype(BF16), w_conv_proj[l].astype(BF16),
                   w_out[l].astype(BF16), mix_post_g[l][None])
        wu = w_up[l].astype(BF16)
        x = _ffn(x, ffn_pre_g[l][None], _chunk_cols(wu[:, :d_ff], n_ff),
                 _chunk_cols(wu[:, d_ff:], n_ff),
                 ffn_conv_w[l].reshape(3, n_ff, FF_CHUNK).transpose(1, 0, 2),
                 w_down[l].astype(BF16).reshape(n_ff, FF_CHUNK, d), ffn_post_g[l][None])
    return x
```

```python
import jax
import jax.numpy as jnp
from jax import lax
from jax.experimental import pallas as pl
from jax.experimental.pallas import tpu as pltpu

N_HEADS = 8
N_KV_HEADS = 2
HEAD_DIM = 128
GROUP = N_HEADS // N_KV_HEADS
GRID_W = 64
ROPE_THETA = 10000.0
AXIS_ROPE_DIM = HEAD_DIM // 2
ROT_HALF = AXIS_ROPE_DIM // 2
EPS = 1e-6
SCALE = HEAD_DIM ** -0.5
LOG2_E = 1.4426950408889634

SUBLANES = 8
HALO = 16
V7X_VMEM_LIMIT_BYTES = 56 * 1024 * 1024

ROW_TILE = 512
Q_ROWS = 512
FF_CHUNK = 256

F32 = jnp.float32
BF16 = jnp.bfloat16


def _rms_scale(v):
    return lax.rsqrt(jnp.mean(v * v, axis=-1, keepdims=True) + EPS)


def _sigmoid(v):
    return 1.0 / (1.0 + jnp.exp(-v))


def _gelu_tanh(v):
    c = 0.7978845608028654
    return 0.5 * v * (1.0 + jnp.tanh(c * (v + 0.044715 * (v * v * v))))


def _attention_kernel(x_ref, g_ref, w_ref, qg_t_ref, kg_ref, cos_ref, sin_ref, cos_t_ref,
                      sin_t_ref, o_ref, qt_scr, k_scr, vt_scr, s_even, s_odd):
    seq = x_ref.shape[1]
    n_tiles = seq // Q_ROWS
    n_items = N_HEADS * n_tiles
    tile_bits = n_tiles.bit_length() - 1
    group_bits = GROUP.bit_length() - 1
    q_w = N_HEADS * HEAD_DIM
    kv_w = N_KV_HEADS * HEAD_DIM
    lane = lax.broadcasted_iota(jnp.int32, (Q_ROWS, HEAD_DIM), 1)
    first_half = (lane & ROT_HALF) == 0

    def rope_rows(z, gain, cos, sin):
        zn = z * _rms_scale(z) * gain
        rot = jnp.where(first_half,
                        pltpu.roll(zn, HEAD_DIM - ROT_HALF, 1),
                        pltpu.roll(zn, ROT_HALF, 1))
        return zn * cos + rot * sin

    def rope_cols(zt, gain_t, cos_t, sin_t):
        inv = lax.rsqrt(jnp.mean(zt * zt, axis=0, keepdims=True) + EPS)
        zn = zt * inv * gain_t
        r = ROT_HALF
        rot = jnp.concatenate([zn[r:2 * r], zn[0:r], zn[3 * r:4 * r], zn[2 * r:3 * r]], axis=0)
        return zn * cos_t + rot * sin_t

    def proj_step(r):
        rows = pl.ds(r * Q_ROWS, Q_ROWS)
        xt = x_ref[0, rows, :]
        h = (xt * _rms_scale(xt) * g_ref[...]).astype(BF16)
        z = jnp.dot(h, w_ref[...], preferred_element_type=F32)
        cos_t = cos_t_ref[r]
        sin_t = sin_t_ref[r]
        for hh in range(N_HEADS):
            zq_t = z[:, hh * HEAD_DIM:(hh + 1) * HEAD_DIM].T
            q_t = rope_cols(zq_t, qg_t_ref[...], cos_t, sin_t) * (SCALE * LOG2_E)
            qt_scr[hh, r] = q_t.astype(BF16)
        cos = cos_ref[rows, :]
        sin = sin_ref[rows, :]
        for kh in range(N_KV_HEADS):
            zk = z[:, q_w + kh * HEAD_DIM:q_w + (kh + 1) * HEAD_DIM]
            k_scr[kh, rows, :] = rope_rows(zk, kg_ref[...], cos, sin).astype(BF16)
            zv = z[:, q_w + kv_w + kh * HEAD_DIM:q_w + kv_w + (kh + 1) * HEAD_DIM]
            vt_scr[kh, r] = zv.T.astype(BF16)

    for r in range(n_tiles):
        proj_step(r)

    def split(item):
        hh = lax.shift_right_logical(item, tile_bits)
        t = item & (n_tiles - 1)
        kh = lax.shift_right_logical(hh, group_bits)
        return hh, t, kh

    def scores(item, s_scr):
        hh, t, kh = split(item)
        s_scr[...] = jnp.dot(k_scr[kh], qt_scr[hh, t], preferred_element_type=F32)

    def finish(item, s_scr):
        hh, t, kh = split(item)
        s_t = s_scr[...]
        m = jnp.max(s_t, axis=0, keepdims=True)
        p_t = jnp.exp2(s_t - m)
        denom = jnp.sum(p_t, axis=0, keepdims=True)
        p16 = p_t.astype(BF16)
        o_t = jnp.dot(vt_scr[kh, 0], p16[0:Q_ROWS], preferred_element_type=F32)
        for kt in range(1, n_tiles):
            o_t = o_t + jnp.dot(vt_scr[kh, kt], p16[kt * Q_ROWS:(kt + 1) * Q_ROWS],
                                preferred_element_type=F32)
        o = (o_t * (1.0 / denom)).T
        rows = pl.ds(pl.multiple_of(t * Q_ROWS, Q_ROWS), Q_ROWS)
        o_ref[0, hh, rows, :] = o.astype(BF16)

    scores(0, s_even)

    def pair_step(j, carry):
        scores(2 * j + 1, s_odd)
        finish(2 * j, s_even)
        scores(2 * j + 2, s_even)
        finish(2 * j + 1, s_odd)
        return carry

    lax.fori_loop(0, n_items // 2 - 1, pair_step, 0)
    scores(n_items - 1, s_odd)
    finish(n_items - 2, s_even)
    finish(n_items - 1, s_odd)


def _attention(x, pre_g, w_qkv, qg_t, k_g, cos, sin, cos_t, sin_t):
    b, s, d = x.shape
    n_tiles = s // Q_ROWS
    const2 = lambda i: (0, 0)
    const3 = lambda i: (0, 0, 0)
    once = pl.Buffered(1)
    return pl.pallas_call(
        _attention_kernel,
        out_shape=jax.ShapeDtypeStruct((b, N_HEADS, s, HEAD_DIM), BF16),
        grid=(b,),
        in_specs=[
            pl.BlockSpec((1, s, d), lambda i: (i, 0, 0), pipeline_mode=once),
            pl.BlockSpec((1, d), const2),
            pl.BlockSpec(w_qkv.shape, const2, pipeline_mode=once),
            pl.BlockSpec(qg_t.shape, const2),
            pl.BlockSpec((1, HEAD_DIM), const2),
            pl.BlockSpec(cos.shape, const2, pipeline_mode=once),
            pl.BlockSpec(sin.shape, const2, pipeline_mode=once),
            pl.BlockSpec(cos_t.shape, const3, pipeline_mode=once),
            pl.BlockSpec(sin_t.shape, const3, pipeline_mode=once),
        ],
        out_specs=pl.BlockSpec((1, N_HEADS, s, HEAD_DIM), lambda i: (i, 0, 0, 0)),
        scratch_shapes=[
            pltpu.VMEM((N_HEADS, n_tiles, HEAD_DIM, Q_ROWS), BF16),
            pltpu.VMEM((N_KV_HEADS, s, HEAD_DIM), BF16),
            pltpu.VMEM((N_KV_HEADS, n_tiles, HEAD_DIM, Q_ROWS), BF16),
            pltpu.VMEM((s, Q_ROWS), F32),
            pltpu.VMEM((s, Q_ROWS), F32),
        ],
        compiler_params=pltpu.CompilerParams(
            dimension_semantics=("arbitrary",), vmem_limit_bytes=V7X_VMEM_LIMIT_BYTES),
        name="attention",
    )(x, pre_g, w_qkv, qg_t, k_g, cos, sin, cos_t, sin_t)


def _halo_specs(seq, d):
    per_tile = ROW_TILE // HALO
    last = seq // HALO - 1
    main = pl.BlockSpec((1, ROW_TILE, d), lambda b, i: (b, i, 0))
    prev = pl.BlockSpec((1, HALO, d), lambda b, i: (b, jnp.maximum(i * per_tile - 1, 0), 0))
    nxt = pl.BlockSpec((1, HALO, d), lambda b, i: (b, jnp.minimum((i + 1) * per_tile, last), 0))
    return main, prev, nxt


def _normed_with_halo(h_scr, x_ref, xp_ref, xn_ref, g_ref):
    i = pl.program_id(1)
    gain = g_ref[...]

    def normed(v):
        return v * _rms_scale(v) * gain

    keep_prev = (i > 0).astype(F32)
    keep_next = (i < pl.num_programs(1) - 1).astype(F32)
    h_scr[0:HALO, :] = (normed(xp_ref[0]) * keep_prev).astype(BF16)
    h_scr[HALO:HALO + ROW_TILE, :] = normed(x_ref[0]).astype(BF16)
    h_scr[HALO + ROW_TILE:, :] = (normed(xn_ref[0]) * keep_next).astype(BF16)


def _conv3(ext, w):
    n = ROW_TILE // SUBLANES
    c = ext.shape[1]
    win = ext[HALO - SUBLANES:HALO + ROW_TILE + SUBLANES].reshape(n + 2, SUBLANES, c)
    sub = lax.broadcasted_iota(jnp.int32, (n, SUBLANES, c), 1)
    down = pltpu.roll(win, 1, 1)
    up = pltpu.roll(win, SUBLANES - 1, 1)
    prev = jnp.where(sub == 0, down[0:n], down[1:n + 1])
    nxt = jnp.where(sub == SUBLANES - 1, up[2:n + 2], up[1:n + 1])
    out = prev * w[0:1, :][None] + win[1:n + 1] * w[1:2, :][None] + nxt * w[2:3, :][None]
    return out.reshape(ROW_TILE, c)


def _mixer_kernel(x_ref, xp_ref, xn_ref, o_ref, pre_g_ref, w_u_ref, w_c_ref, w_b_ref,
                  w_ga_ref, w_gb_ref, gate_b_ref, conv_w_ref, w_attn_ref, w_conv_ref,
                  w_out_ref, post_g_ref, out_ref, h_scr):
    d = x_ref.shape[2]
    _normed_with_halo(h_scr, x_ref, xp_ref, xn_ref, pre_g_ref)
    h_ext = h_scr[...]
    u = jnp.dot(h_ext, w_u_ref[...], preferred_element_type=F32)
    c = jnp.dot(h_ext, w_c_ref[...], preferred_element_type=F32)
    h = h_scr[pl.ds(HALO, ROW_TILE), :]
    b_gate = jnp.dot(h, w_b_ref[...], preferred_element_type=F32)
    conv = (b_gate * _conv3(c * u, conv_w_ref[...])).astype(BF16)
    y_b = jnp.dot(conv, w_conv_ref[...], preferred_element_type=F32)
    attn = jnp.concatenate([o_ref[0, hh] for hh in range(N_HEADS)], axis=1)
    y_a = jnp.dot(attn, w_attn_ref[...], preferred_element_type=F32)
    g_a = _sigmoid(jnp.dot(h, w_ga_ref[...], preferred_element_type=F32) + gate_b_ref[:, :d])
    g_b = _sigmoid(jnp.dot(h, w_gb_ref[...], preferred_element_type=F32) + gate_b_ref[:, d:])
    merged = (g_a * y_a + g_b * y_b).astype(BF16)
    out = jnp.dot(merged, w_out_ref[...], preferred_element_type=F32)
    out_ref[0] = x_ref[0] + out * _rms_scale(out) * post_g_ref[...]


def _mixer(x, o, pre_g, w_u, w_c, w_b, w_ga, w_gb, gate_b, conv_w, w_attn, w_conv, w_out, post_g):
    b, s, d = x.shape
    main, prev, nxt = _halo_specs(s, d)
    const = lambda bi, i: (0, 0)
    full = lambda a: pl.BlockSpec(a.shape, const)
    return pl.pallas_call(
        _mixer_kernel,
        out_shape=jax.ShapeDtypeStruct((b, s, d), F32),
        grid=(b, s // ROW_TILE),
        in_specs=[main, prev, nxt,
                  pl.BlockSpec((1, N_HEADS, ROW_TILE, HEAD_DIM), lambda bi, i: (bi, 0, i, 0)),
                  full(pre_g), full(w_u), full(w_c), full(w_b), full(w_ga), full(w_gb),
                  full(gate_b), full(conv_w), full(w_attn), full(w_conv), full(w_out),
                  full(post_g)],
        out_specs=pl.BlockSpec((1, ROW_TILE, d), lambda bi, i: (bi, i, 0)),
        scratch_shapes=[pltpu.VMEM((ROW_TILE + 2 * HALO, d), BF16)],
        compiler_params=pltpu.CompilerParams(
            dimension_semantics=("arbitrary", "arbitrary"),
            vmem_limit_bytes=V7X_VMEM_LIMIT_BYTES),
        name="mixer",
    )(x, x, x, o, pre_g, w_u, w_c, w_b, w_ga, w_gb, gate_b, conv_w, w_attn, w_conv, w_out, post_g)


def _ffn_kernel(x_ref, xp_ref, xn_ref, pre_g_ref, w_up_ref, conv_w_ref, w_d_ref, post_g_ref,
                out_ref, h_scr, hid_scr):
    d_ff = conv_w_ref.shape[1]
    _normed_with_halo(h_scr, x_ref, xp_ref, xn_ref, pre_g_ref)
    h_ext = h_scr[...]
    h = h_scr[pl.ds(HALO, ROW_TILE), :]
    for j in range(d_ff // FF_CHUNK):
        lo, hi = j * FF_CHUNK, (j + 1) * FF_CHUNK
        a_ext = jnp.dot(h_ext, w_up_ref[:, lo:hi], preferred_element_type=F32)
        val = jnp.dot(h, w_up_ref[:, d_ff + lo:d_ff + hi], preferred_element_type=F32)
        gate = _gelu_tanh(_conv3(a_ext, conv_w_ref[:, lo:hi]))
        hid_scr[:, lo:hi] = (gate * val).astype(BF16)
    out = jnp.dot(hid_scr[...], w_d_ref[...], preferred_element_type=F32)
    out_ref[0] = x_ref[0] + out * _rms_scale(out) * post_g_ref[...]


def _ffn(x, pre_g, w_up, conv_w, w_d, post_g):
    b, s, d = x.shape
    main, prev, nxt = _halo_specs(s, d)
    const2 = lambda bi, i: (0, 0)
    once = pl.Buffered(1)
    return pl.pallas_call(
        _ffn_kernel,
        out_shape=jax.ShapeDtypeStruct((b, s, d), F32),
        grid=(b, s // ROW_TILE),
        in_specs=[main, prev, nxt,
                  pl.BlockSpec(pre_g.shape, const2),
                  pl.BlockSpec(w_up.shape, const2, pipeline_mode=once),
                  pl.BlockSpec(conv_w.shape, const2),
                  pl.BlockSpec(w_d.shape, const2, pipeline_mode=once),
                  pl.BlockSpec(post_g.shape, const2)],
        out_specs=pl.BlockSpec((1, ROW_TILE, d), lambda bi, i: (bi, i, 0)),
        scratch_shapes=[
            pltpu.VMEM((ROW_TILE + 2 * HALO, d), BF16),
            pltpu.VMEM((ROW_TILE, w_d.shape[0]), BF16),
        ],
        compiler_params=pltpu.CompilerParams(
            dimension_semantics=("arbitrary", "arbitrary"),
            vmem_limit_bytes=V7X_VMEM_LIMIT_BYTES),
        name="ffn",
    )(x, x, x, pre_g, w_up, conv_w, w_d, post_g)


def _rope_tables(seq):
    t = jnp.arange(seq)
    freqs = ROPE_THETA ** (-jnp.arange(ROT_HALF, dtype=F32) / ROT_HALF)

    def axis_tables(pos):
        ang = pos.astype(F32)[:, None] * freqs[None, :]
        ang = jnp.concatenate([ang, ang], axis=-1)
        return jnp.cos(ang), jnp.sin(ang)

    cos_r, sin_r = axis_tables(t // GRID_W)
    cos_c, sin_c = axis_tables(t % GRID_W)
    sign = jnp.where(jnp.arange(AXIS_ROPE_DIM) < ROT_HALF, -1.0, 1.0).astype(F32)
    cos = jnp.concatenate([cos_r, cos_c], axis=-1)
    sin = jnp.concatenate([sin_r * sign, sin_c * sign], axis=-1)
    return cos, sin


def _tile_transposed(table):
    seq = table.shape[0]
    return table.reshape(seq // Q_ROWS, Q_ROWS, HEAD_DIM).transpose(0, 2, 1)


def kernel(x, mix_pre_g, w_in, gate_b, q_norm_g, k_norm_g, mix_conv_w, w_attn_proj, w_conv_proj,
           w_out, mix_post_g, ffn_pre_g, w_up, ffn_conv_w, w_down, ffn_post_g):
    depth = w_in.shape[0]
    seq, d = x.shape[1], x.shape[2]
    q_w = N_HEADS * HEAD_DIM
    kv_w = N_KV_HEADS * HEAD_DIM
    d_conv = mix_conv_w.shape[2]
    cos, sin = _rope_tables(seq)
    cos_t, sin_t = _tile_transposed(cos), _tile_transposed(sin)
    for l in range(depth):
        w = w_in[l].astype(BF16)
        o0 = q_w + 2 * kv_w
        w_qkv = w[:, :o0]
        w_u = w[:, o0:o0 + d_conv]
        w_b = w[:, o0 + d_conv:o0 + 2 * d_conv]
        w_c = w[:, o0 + 2 * d_conv:o0 + 3 * d_conv]
        w_ga = w[:, o0 + 3 * d_conv:o0 + 3 * d_conv + d]
        w_gb = w[:, o0 + 3 * d_conv + d:]
        qg_t = jnp.broadcast_to(q_norm_g[l][:, None], (HEAD_DIM, Q_ROWS))
        attn = _attention(x, mix_pre_g[l][None], w_qkv, qg_t, k_norm_g[l][None],
                          cos, sin, cos_t, sin_t)
        x = _mixer(x, attn, mix_pre_g[l][None], w_u, w_c, w_b, w_ga, w_gb, gate_b[l][None],
                   mix_conv_w[l], w_attn_proj[l].astype(BF16), w_conv_proj[l].astype(BF16),
                   w_out[l].astype(BF16), mix_post_g[l][None])
        x = _ffn(x, ffn_pre_g[l][None], w_up[l].astype(BF16), ffn_conv_w[l],
                 w_down[l].astype(BF16), ffn_post_g[l][None])
    return x
```

```python
import jax
import jax.numpy as jnp
from jax import lax
from jax.experimental import pallas as pl
from jax.experimental.pallas import tpu as pltpu

N_HEADS = 8
N_KV_HEADS = 2
HEAD_DIM = 128
GROUP = N_HEADS // N_KV_HEADS
GRID_W = 64
ROPE_THETA = 10000.0
AXIS_ROPE_DIM = HEAD_DIM // 2
ROT_HALF = AXIS_ROPE_DIM // 2
EPS = 1e-6
SCALE = HEAD_DIM ** -0.5
LOG2_E = 1.4426950408889634

SUBLANES = 8
HALO = 16
ONES_ROWS = 16
V7X_VMEM_LIMIT_BYTES = 56 * 1024 * 1024

ROW_TILE = 512
Q_ROWS = 512
FF_CHUNK = 256

F32 = jnp.float32
BF16 = jnp.bfloat16


def _rms_scale(v):
    return lax.rsqrt(jnp.mean(v * v, axis=-1, keepdims=True) + EPS)


def _sigmoid(v):
    return 1.0 / (1.0 + jnp.exp(-v))


def _gelu_tanh(v):
    c = 0.7978845608028654
    return 0.5 * v * (1.0 + jnp.tanh(c * (v + 0.044715 * (v * v * v))))


def _attention_kernel(x_hbm, g_ref, w_ref, qg_t_ref, kg_ref, cos_ref, sin_ref, cos_t_ref,
                      sin_t_ref, o_ref, qt_scr, k_scr, vt_scr, s_even, s_odd, m_even, m_odd,
                      x_buf, x_sem):
    seq = x_hbm.shape[1]
    n_tiles = seq // Q_ROWS
    batch = pl.program_id(0)
    n_items = N_HEADS * n_tiles
    tile_bits = n_tiles.bit_length() - 1
    group_bits = GROUP.bit_length() - 1
    q_w = N_HEADS * HEAD_DIM
    kv_w = N_KV_HEADS * HEAD_DIM
    lane = lax.broadcasted_iota(jnp.int32, (Q_ROWS, HEAD_DIM), 1)
    first_half = (lane & ROT_HALF) == 0

    def rope_rows(z, gain, cos, sin):
        zn = z * _rms_scale(z) * gain
        rot = jnp.where(first_half,
                        pltpu.roll(zn, HEAD_DIM - ROT_HALF, 1),
                        pltpu.roll(zn, ROT_HALF, 1))
        return zn * cos + rot * sin

    def rope_cols(zt, gain_t, cos_t, sin_t):
        inv = lax.rsqrt(jnp.mean(zt * zt, axis=0, keepdims=True) + EPS)
        zn = zt * inv * gain_t
        r = ROT_HALF
        rot = jnp.concatenate([zn[r:2 * r], zn[0:r], zn[3 * r:4 * r], zn[2 * r:3 * r]], axis=0)
        return zn * cos_t + rot * sin_t

    def x_copy(bi):
        return pltpu.make_async_copy(x_hbm.at[bi], x_buf, x_sem.at[0])

    def proj_step(r):
        rows = pl.ds(r * Q_ROWS, Q_ROWS)
        xt = x_buf[rows, :]
        h = (xt * _rms_scale(xt) * g_ref[...]).astype(BF16)
        z = jnp.dot(h, w_ref[...], preferred_element_type=F32)
        cos_t = cos_t_ref[r]
        sin_t = sin_t_ref[r]
        for hh in range(N_HEADS):
            zq_t = z[:, hh * HEAD_DIM:(hh + 1) * HEAD_DIM].T
            q_t = rope_cols(zq_t, qg_t_ref[...], cos_t, sin_t) * (SCALE * LOG2_E)
            qt_scr[hh, r] = q_t.astype(BF16)
        cos = cos_ref[rows, :]
        sin = sin_ref[rows, :]
        for kh in range(N_KV_HEADS):
            zk = z[:, q_w + kh * HEAD_DIM:q_w + (kh + 1) * HEAD_DIM]
            k_scr[kh, rows, :] = rope_rows(zk, kg_ref[...], cos, sin).astype(BF16)
            zv = z[:, q_w + kv_w + kh * HEAD_DIM:q_w + kv_w + (kh + 1) * HEAD_DIM]
            vt_scr[kh, r, 0:HEAD_DIM, :] = zv.T.astype(BF16)
            vt_scr[kh, r, HEAD_DIM:, :] = jnp.ones((ONES_ROWS, Q_ROWS), BF16)

    @pl.when(batch == 0)
    def _():
        x_copy(0).start()

    x_copy(batch).wait()
    for r in range(n_tiles):
        proj_step(r)

    @pl.when(batch + 1 < pl.num_programs(0))
    def _():
        x_copy(batch + 1).start()

    def split(item):
        hh = lax.shift_right_logical(item, tile_bits)
        t = item & (n_tiles - 1)
        kh = lax.shift_right_logical(hh, group_bits)
        return hh, t, kh

    def scores(item, s_scr, m_scr):
        hh, t, kh = split(item)
        s_t = jnp.dot(k_scr[kh], qt_scr[hh, t], preferred_element_type=F32)
        s_scr[...] = s_t
        m_scr[...] = jnp.max(s_t, axis=0, keepdims=True)

    def finish(item, s_scr, m_scr):
        hh, t, kh = split(item)
        p16 = jnp.exp2(s_scr[...] - m_scr[...]).astype(BF16)
        acc = jnp.dot(vt_scr[kh, 0], p16[0:Q_ROWS], preferred_element_type=F32)
        for kt in range(1, n_tiles):
            acc = acc + jnp.dot(vt_scr[kh, kt], p16[kt * Q_ROWS:(kt + 1) * Q_ROWS],
                                preferred_element_type=F32)
        denom = acc[HEAD_DIM:HEAD_DIM + 1]
        o = (acc[0:HEAD_DIM] * (1.0 / denom)).T
        rows = pl.ds(pl.multiple_of(t * Q_ROWS, Q_ROWS), Q_ROWS)
        o_ref[0, hh, rows, :] = o.astype(BF16)

    scores(0, s_even, m_even)

    def pair_step(j, carry):
        scores(2 * j + 1, s_odd, m_odd)
        finish(2 * j, s_even, m_even)
        scores(2 * j + 2, s_even, m_even)
        finish(2 * j + 1, s_odd, m_odd)
        return carry

    lax.fori_loop(0, n_items // 2 - 1, pair_step, 0)
    scores(n_items - 1, s_odd, m_odd)
    finish(n_items - 2, s_even, m_even)
    finish(n_items - 1, s_odd, m_odd)


def _attention(x, pre_g, w_qkv, qg_t, k_g, cos, sin, cos_t, sin_t):
    b, s, d = x.shape
    n_tiles = s // Q_ROWS
    const2 = lambda i: (0, 0)
    const3 = lambda i: (0, 0, 0)
    once = pl.Buffered(1)
    return pl.pallas_call(
        _attention_kernel,
        out_shape=jax.ShapeDtypeStruct((b, N_HEADS, s, HEAD_DIM), BF16),
        grid=(b,),
        in_specs=[
            pl.BlockSpec(memory_space=pl.ANY),
            pl.BlockSpec((1, d), const2),
            pl.BlockSpec(w_qkv.shape, const2, pipeline_mode=once),
            pl.BlockSpec(qg_t.shape, const2),
            pl.BlockSpec((1, HEAD_DIM), const2),
            pl.BlockSpec(cos.shape, const2, pipeline_mode=once),
            pl.BlockSpec(sin.shape, const2, pipeline_mode=once),
            pl.BlockSpec(cos_t.shape, const3, pipeline_mode=once),
            pl.BlockSpec(sin_t.shape, const3, pipeline_mode=once),
        ],
        out_specs=pl.BlockSpec((1, N_HEADS, s, HEAD_DIM), lambda i: (i, 0, 0, 0)),
        scratch_shapes=[
            pltpu.VMEM((N_HEADS, n_tiles, HEAD_DIM, Q_ROWS), BF16),
            pltpu.VMEM((N_KV_HEADS, s, HEAD_DIM), BF16),
            pltpu.VMEM((N_KV_HEADS, n_tiles, HEAD_DIM + ONES_ROWS, Q_ROWS), BF16),
            pltpu.VMEM((s, Q_ROWS), F32),
            pltpu.VMEM((s, Q_ROWS), F32),
            pltpu.VMEM((1, Q_ROWS), F32),
            pltpu.VMEM((1, Q_ROWS), F32),
            pltpu.VMEM((s, d), F32),
            pltpu.SemaphoreType.DMA((1,)),
        ],
        compiler_params=pltpu.CompilerParams(
            dimension_semantics=("arbitrary",), vmem_limit_bytes=V7X_VMEM_LIMIT_BYTES),
        name="attention",
    )(x, pre_g, w_qkv, qg_t, k_g, cos, sin, cos_t, sin_t)


def _halo_specs(seq, d):
    per_tile = ROW_TILE // HALO
    last = seq // HALO - 1
    main = pl.BlockSpec((1, ROW_TILE, d), lambda b, i: (b, i, 0))
    prev = pl.BlockSpec((1, HALO, d), lambda b, i: (b, jnp.maximum(i * per_tile - 1, 0), 0))
    nxt = pl.BlockSpec((1, HALO, d), lambda b, i: (b, jnp.minimum((i + 1) * per_tile, last), 0))
    return main, prev, nxt


def _normed_with_halo(h_scr, x_ref, xp_ref, xn_ref, g_ref):
    i = pl.program_id(1)
    gain = g_ref[...]

    def normed(v):
        return v * _rms_scale(v) * gain

    keep_prev = (i > 0).astype(F32)
    keep_next = (i < pl.num_programs(1) - 1).astype(F32)
    h_scr[0:HALO, :] = (normed(xp_ref[0]) * keep_prev).astype(BF16)
    h_scr[HALO:HALO + ROW_TILE, :] = normed(x_ref[0]).astype(BF16)
    h_scr[HALO + ROW_TILE:, :] = (normed(xn_ref[0]) * keep_next).astype(BF16)


def _conv3(ext, w):
    n = ROW_TILE // SUBLANES
    c = ext.shape[1]
    win = ext[HALO - SUBLANES:HALO + ROW_TILE + SUBLANES].reshape(n + 2, SUBLANES, c)
    sub = lax.broadcasted_iota(jnp.int32, (n, SUBLANES, c), 1)
    down = pltpu.roll(win, 1, 1)
    up = pltpu.roll(win, SUBLANES - 1, 1)
    prev = jnp.where(sub == 0, down[0:n], down[1:n + 1])
    nxt = jnp.where(sub == SUBLANES - 1, up[2:n + 2], up[1:n + 1])
    out = prev * w[0:1, :][None] + win[1:n + 1] * w[1:2, :][None] + nxt * w[2:3, :][None]
    return out.reshape(ROW_TILE, c)


def _mixer_kernel(x_ref, xp_ref, xn_ref, o_ref, pre_g_ref, w_u_ref, w_c_ref, w_b_ref,
                  w_ga_ref, w_gb_ref, gate_b_ref, conv_w_ref, w_attn_ref, w_conv_ref,
                  w_out_ref, post_g_ref, out_ref, h_scr):
    d = x_ref.shape[2]
    _normed_with_halo(h_scr, x_ref, xp_ref, xn_ref, pre_g_ref)
    h_ext = h_scr[...]
    u = jnp.dot(h_ext, w_u_ref[...], preferred_element_type=F32)
    c = jnp.dot(h_ext, w_c_ref[...], preferred_element_type=F32)
    h = h_scr[pl.ds(HALO, ROW_TILE), :]
    b_gate = jnp.dot(h, w_b_ref[...], preferred_element_type=F32)
    conv = (b_gate * _conv3(c * u, conv_w_ref[...])).astype(BF16)
    y_b = jnp.dot(conv, w_conv_ref[...], preferred_element_type=F32)
    attn = jnp.concatenate([o_ref[0, hh] for hh in range(N_HEADS)], axis=1)
    y_a = jnp.dot(attn, w_attn_ref[...], preferred_element_type=F32)
    g_a = _sigmoid(jnp.dot(h, w_ga_ref[...], preferred_element_type=F32) + gate_b_ref[:, :d])
    g_b = _sigmoid(jnp.dot(h, w_gb_ref[...], preferred_element_type=F32) + gate_b_ref[:, d:])
    merged = (g_a * y_a + g_b * y_b).astype(BF16)
    out = jnp.dot(merged, w_out_ref[...], preferred_element_type=F32)
    out_ref[0] = x_ref[0] + out * _rms_scale(out) * post_g_ref[...]


def _mixer(x, o, pre_g, w_u, w_c, w_b, w_ga, w_gb, gate_b, conv_w, w_attn, w_conv, w_out, post_g):
    b, s, d = x.shape
    main, prev, nxt = _halo_specs(s, d)
    const = lambda bi, i: (0, 0)
    full = lambda a: pl.BlockSpec(a.shape, const)
    return pl.pallas_call(
        _mixer_kernel,
        out_shape=jax.ShapeDtypeStruct((b, s, d), F32),
        grid=(b, s // ROW_TILE),
        in_specs=[main, prev, nxt,
                  pl.BlockSpec((1, N_HEADS, ROW_TILE, HEAD_DIM), lambda bi, i: (bi, 0, i, 0)),
                  full(pre_g), full(w_u), full(w_c), full(w_b), full(w_ga), full(w_gb),
                  full(gate_b), full(conv_w), full(w_attn), full(w_conv), full(w_out),
                  full(post_g)],
        out_specs=pl.BlockSpec((1, ROW_TILE, d), lambda bi, i: (bi, i, 0)),
        scratch_shapes=[pltpu.VMEM((ROW_TILE + 2 * HALO, d), BF16)],
        compiler_params=pltpu.CompilerParams(
            dimension_semantics=("arbitrary", "arbitrary"),
            vmem_limit_bytes=V7X_VMEM_LIMIT_BYTES),
        name="mixer",
    )(x, x, x, o, pre_g, w_u, w_c, w_b, w_ga, w_gb, gate_b, conv_w, w_attn, w_conv, w_out, post_g)


def _ffn_kernel(x_ref, xp_ref, xn_ref, pre_g_ref, w_up_ref, conv_w_ref, w_d_ref, post_g_ref,
                out_ref, h_scr, hid_scr):
    d_ff = conv_w_ref.shape[1]
    _normed_with_halo(h_scr, x_ref, xp_ref, xn_ref, pre_g_ref)
    h_ext = h_scr[...]
    h = h_scr[pl.ds(HALO, ROW_TILE), :]
    for j in range(d_ff // FF_CHUNK):
        lo, hi = j * FF_CHUNK, (j + 1) * FF_CHUNK
        a_ext = jnp.dot(h_ext, w_up_ref[:, lo:hi], preferred_element_type=F32)
        val = jnp.dot(h, w_up_ref[:, d_ff + lo:d_ff + hi], preferred_element_type=F32)
        gate = _gelu_tanh(_conv3(a_ext, conv_w_ref[:, lo:hi]))
        hid_scr[:, lo:hi] = (gate * val).astype(BF16)
    out = jnp.dot(hid_scr[...], w_d_ref[...], preferred_element_type=F32)
    out_ref[0] = x_ref[0] + out * _rms_scale(out) * post_g_ref[...]


def _ffn(x, pre_g, w_up, conv_w, w_d, post_g):
    b, s, d = x.shape
    main, prev, nxt = _halo_specs(s, d)
    const2 = lambda bi, i: (0, 0)
    once = pl.Buffered(1)
    return pl.pallas_call(
        _ffn_kernel,
        out_shape=jax.ShapeDtypeStruct((b, s, d), F32),
        grid=(b, s // ROW_TILE),
        in_specs=[main, prev, nxt,
                  pl.BlockSpec(pre_g.shape, const2),
                  pl.BlockSpec(w_up.shape, const2, pipeline_mode=once),
                  pl.BlockSpec(conv_w.shape, const2),
                  pl.BlockSpec(w_d.shape, const2, pipeline_mode=once),
                  pl.BlockSpec(post_g.shape, const2)],
        out_specs=pl.BlockSpec((1, ROW_TILE, d), lambda bi, i: (bi, i, 0)),
        scratch_shapes=[
            pltpu.VMEM((ROW_TILE + 2 * HALO, d), BF16),
            pltpu.VMEM((ROW_TILE, w_d.shape[0]), BF16),
        ],
        compiler_params=pltpu.CompilerParams(
            dimension_semantics=("arbitrary", "arbitrary"),
            vmem_limit_bytes=V7X_VMEM_LIMIT_BYTES),
        name="ffn",
    )(x, x, x, pre_g, w_up, conv_w, w_d, post_g)


def _rope_tables(seq):
    t = jnp.arange(seq)
    freqs = ROPE_THETA ** (-jnp.arange(ROT_HALF, dtype=F32) / ROT_HALF)

    def axis_tables(pos):
        ang = pos.astype(F32)[:, None] * freqs[None, :]
        ang = jnp.concatenate([ang, ang], axis=-1)
        return jnp.cos(ang), jnp.sin(ang)

    cos_r, sin_r = axis_tables(t // GRID_W)
    cos_c, sin_c = axis_tables(t % GRID_W)
    sign = jnp.where(jnp.arange(AXIS_ROPE_DIM) < ROT_HALF, -1.0, 1.0).astype(F32)
    cos = jnp.concatenate([cos_r, cos_c], axis=-1)
    sin = jnp.concatenate([sin_r * sign, sin_c * sign], axis=-1)
    return cos, sin


def _tile_transposed(table):
    seq = table.shape[0]
    return table.reshape(seq // Q_ROWS, Q_ROWS, HEAD_DIM).transpose(0, 2, 1)


def kernel(x, mix_pre_g, w_in, gate_b, q_norm_g, k_norm_g, mix_conv_w, w_attn_proj, w_conv_proj,
           w_out, mix_post_g, ffn_pre_g, w_up, ffn_conv_w, w_down, ffn_post_g):
    depth = w_in.shape[0]
    seq, d = x.shape[1], x.shape[2]
    q_w = N_HEADS * HEAD_DIM
    kv_w = N_KV_HEADS * HEAD_DIM
    d_conv = mix_conv_w.shape[2]
    cos, sin = _rope_tables(seq)
    cos_t, sin_t = _tile_transposed(cos), _tile_transposed(sin)
    for l in range(depth):
        w = w_in[l].astype(BF16)
        o0 = q_w + 2 * kv_w
        w_qkv = w[:, :o0]
        w_u = w[:, o0:o0 + d_conv]
        w_b = w[:, o0 + d_conv:o0 + 2 * d_conv]
        w_c = w[:, o0 + 2 * d_conv:o0 + 3 * d_conv]
        w_ga = w[:, o0 + 3 * d_conv:o0 + 3 * d_conv + d]
        w_gb = w[:, o0 + 3 * d_conv + d:]
        qg_t = jnp.broadcast_to(q_norm_g[l][:, None], (HEAD_DIM, Q_ROWS))
        attn = _attention(x, mix_pre_g[l][None], w_qkv, qg_t, k_norm_g[l][None],
                          cos, sin, cos_t, sin_t)
        x = _mixer(x, attn, mix_pre_g[l][None], w_u, w_c, w_b, w_ga, w_gb, gate_b[l][None],
                   mix_conv_w[l], w_attn_proj[l].astype(BF16), w_conv_proj[l].astype(BF16),
                   w_out[l].astype(BF16), mix_post_g[l][None])
        x = _ffn(x, ffn_pre_g[l][None], w_up[l].astype(BF16), ffn_conv_w[l],
                 w_down[l].astype(BF16), ffn_post_g[l][None])
    return x
```

```python
import numpy as np

import jax
import jax.numpy as jnp
from jax import lax
from jax.experimental import pallas as pl
from jax.experimental.pallas import tpu as pltpu

N_HEADS = 8
N_KV_HEADS = 2
HEAD_DIM = 128
GROUP = N_HEADS // N_KV_HEADS
GRID_W = 64
ROPE_THETA = 10000.0
AXIS_ROPE_DIM = HEAD_DIM // 2
ROT_HALF = AXIS_ROPE_DIM // 2
EPS = 1e-6
SCALE = HEAD_DIM ** -0.5
LOG2_E = 1.4426950408889634

SUBLANES = 8
HALO = 16
ONES_ROWS = 16
V7X_VMEM_LIMIT_BYTES = 56 * 1024 * 1024

ROW_TILE = 1024
Q_ROWS = 512
PEELED_PAIRS = 2
FF_CHUNK = 256

F32 = jnp.float32
BF16 = jnp.bfloat16


def _rms_scale(v):
    return lax.rsqrt(jnp.mean(v * v, axis=-1, keepdims=True) + EPS)


def _sigmoid(v):
    return 1.0 / (1.0 + jnp.exp(-v))


def _gelu_tanh(v):
    c = 0.7978845608028654
    return 0.5 * v * (1.0 + jnp.tanh(c * (v + 0.044715 * (v * v * v))))


def _attention_kernel(x_hbm, g_ref, w_ref, qg_t_ref, kg_ref, cos_ref, sin_ref, cos_t_ref,
                      sin_t_ref, o_ref, qt_scr, k_scr, vt_scr, s_even, s_odd, m_even, m_odd,
                      x_buf, x_sem):
    seq = x_hbm.shape[1]
    n_tiles = seq // Q_ROWS
    batch = pl.program_id(0)
    n_items = N_HEADS * n_tiles
    tile_bits = n_tiles.bit_length() - 1
    group_bits = GROUP.bit_length() - 1
    q_w = N_HEADS * HEAD_DIM
    kv_w = N_KV_HEADS * HEAD_DIM
    lane = lax.broadcasted_iota(jnp.int32, (Q_ROWS, HEAD_DIM), 1)
    first_half = (lane & ROT_HALF) == 0

    def rope_rows(z, gain, cos, sin):
        zn = z * _rms_scale(z) * gain
        rot = jnp.where(first_half,
                        pltpu.roll(zn, HEAD_DIM - ROT_HALF, 1),
                        pltpu.roll(zn, ROT_HALF, 1))
        return zn * cos + rot * sin

    def rope_cols(zt, gain_t, cos_t, sin_t):
        inv = lax.rsqrt(jnp.mean(zt * zt, axis=0, keepdims=True) + EPS)
        zn = zt * inv * gain_t
        r = ROT_HALF
        rot = jnp.concatenate([zn[r:2 * r], zn[0:r], zn[3 * r:4 * r], zn[2 * r:3 * r]], axis=0)
        return zn * cos_t + rot * sin_t

    def x_copy(bi):
        return pltpu.make_async_copy(x_hbm.at[bi], x_buf, x_sem.at[0])

    def proj_step(r):
        rows = pl.ds(r * Q_ROWS, Q_ROWS)
        xt = x_buf[rows, :]
        h = (xt * _rms_scale(xt) * g_ref[...]).astype(BF16)
        z = jnp.dot(h, w_ref[...], preferred_element_type=F32)
        cos_t = cos_t_ref[r]
        sin_t = sin_t_ref[r]
        for hh in range(N_HEADS):
            zq_t = z[:, hh * HEAD_DIM:(hh + 1) * HEAD_DIM].T
            q_t = rope_cols(zq_t, qg_t_ref[...], cos_t, sin_t) * (SCALE * LOG2_E)
            qt_scr[hh, r] = q_t.astype(BF16)
        cos = cos_ref[rows, :]
        sin = sin_ref[rows, :]
        for kh in range(N_KV_HEADS):
            zk = z[:, q_w + kh * HEAD_DIM:q_w + (kh + 1) * HEAD_DIM]
            k_scr[kh, rows, :] = rope_rows(zk, kg_ref[...], cos, sin).astype(BF16)
            zv = z[:, q_w + kv_w + kh * HEAD_DIM:q_w + kv_w + (kh + 1) * HEAD_DIM]
            vt_scr[kh, r, 0:HEAD_DIM, :] = zv.T.astype(BF16)
            vt_scr[kh, r, HEAD_DIM:, :] = jnp.ones((ONES_ROWS, Q_ROWS), BF16)

    @pl.when(batch == 0)
    def _():
        x_copy(0).start()

    x_copy(batch).wait()
    for r in range(n_tiles):
        proj_step(r)

    def split(item):
        hh = item >> tile_bits
        t = item & (n_tiles - 1)
        kh = hh >> group_bits
        return hh, t, kh

    def scores(item, s_scr, m_scr):
        hh, t, kh = split(item)
        s_t = jnp.dot(k_scr[kh], qt_scr[hh, t], preferred_element_type=F32)
        s_scr[...] = s_t
        m_scr[...] = jnp.max(s_t, axis=0, keepdims=True)

    def finish(item, s_scr, m_scr):
        hh, t, kh = split(item)
        p16 = jnp.exp2(s_scr[...] - m_scr[...]).astype(BF16)
        acc = jnp.dot(vt_scr[kh, 0], p16[0:Q_ROWS], preferred_element_type=F32)
        for kt in range(1, n_tiles):
            acc = acc + jnp.dot(vt_scr[kh, kt], p16[kt * Q_ROWS:(kt + 1) * Q_ROWS],
                                preferred_element_type=F32)
        denom = acc[HEAD_DIM:HEAD_DIM + 1]
        o = (acc[0:HEAD_DIM] * (1.0 / denom)).T
        start = t * Q_ROWS
        if not isinstance(start, int):
            start = pl.multiple_of(start, Q_ROWS)
        o_ref[0, hh, pl.ds(start, Q_ROWS), :] = o.astype(BF16)

    def pair_step(j, carry=None):
        scores(2 * j + 1, s_odd, m_odd)
        finish(2 * j, s_even, m_even)
        scores(2 * j + 2, s_even, m_even)
        finish(2 * j + 1, s_odd, m_odd)
        return carry

    scores(0, s_even, m_even)
    for j in range(PEELED_PAIRS):
        pair_step(j)

    @pl.when(batch + 1 < pl.num_programs(0))
    def _():
        x_copy(batch + 1).start()

    lax.fori_loop(PEELED_PAIRS, n_items // 2 - 1, pair_step, 0)
    scores(n_items - 1, s_odd, m_odd)
    finish(n_items - 2, s_even, m_even)
    finish(n_items - 1, s_odd, m_odd)


def _attention(x, pre_g, w_qkv, qg_t, k_g, cos, sin, cos_t, sin_t):
    b, s, d = x.shape
    n_tiles = s // Q_ROWS
    const2 = lambda i: (0, 0)
    const3 = lambda i: (0, 0, 0)
    once = pl.Buffered(1)
    return pl.pallas_call(
        _attention_kernel,
        out_shape=jax.ShapeDtypeStruct((b, N_HEADS, s, HEAD_DIM), BF16),
        grid=(b,),
        in_specs=[
            pl.BlockSpec(memory_space=pl.ANY),
            pl.BlockSpec((1, d), const2),
            pl.BlockSpec(w_qkv.shape, const2, pipeline_mode=once),
            pl.BlockSpec(qg_t.shape, const2),
            pl.BlockSpec((1, HEAD_DIM), const2),
            pl.BlockSpec(cos.shape, const2, pipeline_mode=once),
            pl.BlockSpec(sin.shape, const2, pipeline_mode=once),
            pl.BlockSpec(cos_t.shape, const3, pipeline_mode=once),
            pl.BlockSpec(sin_t.shape, const3, pipeline_mode=once),
        ],
        out_specs=pl.BlockSpec((1, N_HEADS, s, HEAD_DIM), lambda i: (i, 0, 0, 0)),
        scratch_shapes=[
            pltpu.VMEM((N_HEADS, n_tiles, HEAD_DIM, Q_ROWS), BF16),
            pltpu.VMEM((N_KV_HEADS, s, HEAD_DIM), BF16),
            pltpu.VMEM((N_KV_HEADS, n_tiles, HEAD_DIM + ONES_ROWS, Q_ROWS), BF16),
            pltpu.VMEM((s, Q_ROWS), F32),
            pltpu.VMEM((s, Q_ROWS), F32),
            pltpu.VMEM((1, Q_ROWS), F32),
            pltpu.VMEM((1, Q_ROWS), F32),
            pltpu.VMEM((s, d), F32),
            pltpu.SemaphoreType.DMA((1,)),
        ],
        compiler_params=pltpu.CompilerParams(
            dimension_semantics=("arbitrary",), vmem_limit_bytes=V7X_VMEM_LIMIT_BYTES),
        name="attention",
    )(x, pre_g, w_qkv, qg_t, k_g, cos, sin, cos_t, sin_t)


def _halo_specs(seq, d):
    per_tile = ROW_TILE // HALO
    last = seq // HALO - 1
    main = pl.BlockSpec((1, ROW_TILE, d), lambda b, i: (b, i, 0))
    prev = pl.BlockSpec((1, HALO, d), lambda b, i: (b, jnp.maximum(i * per_tile - 1, 0), 0))
    nxt = pl.BlockSpec((1, HALO, d), lambda b, i: (b, jnp.minimum((i + 1) * per_tile, last), 0))
    return main, prev, nxt


def _normed_with_halo(h_scr, x_ref, xp_ref, xn_ref, g_ref):
    i = pl.program_id(1)
    gain = g_ref[...]

    def normed(v):
        return v * _rms_scale(v) * gain

    keep_prev = (i > 0).astype(F32)
    keep_next = (i < pl.num_programs(1) - 1).astype(F32)
    h_scr[0:HALO, :] = (normed(xp_ref[0]) * keep_prev).astype(BF16)
    h_scr[HALO:HALO + ROW_TILE, :] = normed(x_ref[0]).astype(BF16)
    h_scr[HALO + ROW_TILE:, :] = (normed(xn_ref[0]) * keep_next).astype(BF16)


def _conv3(ext, w):
    n = ROW_TILE // SUBLANES
    c = ext.shape[1]
    win = ext[HALO - SUBLANES:HALO + ROW_TILE + SUBLANES].reshape(n + 2, SUBLANES, c)
    sub = lax.broadcasted_iota(jnp.int32, (n, SUBLANES, c), 1)
    down = pltpu.roll(win, 1, 1)
    up = pltpu.roll(win, SUBLANES - 1, 1)
    prev = jnp.where(sub == 0, down[0:n], down[1:n + 1])
    nxt = jnp.where(sub == SUBLANES - 1, up[2:n + 2], up[1:n + 1])
    out = prev * w[0:1, :][None] + win[1:n + 1] * w[1:2, :][None] + nxt * w[2:3, :][None]
    return out.reshape(ROW_TILE, c)


def _mixer_kernel(x_ref, xp_ref, xn_ref, o_ref, pre_g_ref, w_u_ref, w_c_ref, w_b_ref,
                  w_ga_ref, w_gb_ref, gate_b_ref, conv_w_ref, w_attn_ref, w_conv_ref,
                  w_out_ref, post_g_ref, out_ref, h_scr):
    d = x_ref.shape[2]
    _normed_with_halo(h_scr, x_ref, xp_ref, xn_ref, pre_g_ref)
    h_ext = h_scr[...]
    u = jnp.dot(h_ext, w_u_ref[...], preferred_element_type=F32)
    c = jnp.dot(h_ext, w_c_ref[...], preferred_element_type=F32)
    h = h_scr[pl.ds(HALO, ROW_TILE), :]
    b_gate = jnp.dot(h, w_b_ref[...], preferred_element_type=F32)
    conv = (b_gate * _conv3(c * u, conv_w_ref[...])).astype(BF16)
    y_b = jnp.dot(conv, w_conv_ref[...], preferred_element_type=F32)
    attn = jnp.concatenate([o_ref[0, hh] for hh in range(N_HEADS)], axis=1)
    y_a = jnp.dot(attn, w_attn_ref[...], preferred_element_type=F32)
    g_a = _sigmoid(jnp.dot(h, w_ga_ref[...], preferred_element_type=F32) + gate_b_ref[:, :d])
    g_b = _sigmoid(jnp.dot(h, w_gb_ref[...], preferred_element_type=F32) + gate_b_ref[:, d:])
    merged = (g_a * y_a + g_b * y_b).astype(BF16)
    out = jnp.dot(merged, w_out_ref[...], preferred_element_type=F32)
    out_ref[0] = x_ref[0] + out * _rms_scale(out) * post_g_ref[...]


def _mixer(x, o, pre_g, w_u, w_c, w_b, w_ga, w_gb, gate_b, conv_w, w_attn, w_conv, w_out, post_g):
    b, s, d = x.shape
    main, prev, nxt = _halo_specs(s, d)
    const = lambda bi, i: (0, 0)
    full = lambda a: pl.BlockSpec(a.shape, const)
    return pl.pallas_call(
        _mixer_kernel,
        out_shape=jax.ShapeDtypeStruct((b, s, d), F32),
        grid=(b, s // ROW_TILE),
        in_specs=[main, prev, nxt,
                  pl.BlockSpec((1, N_HEADS, ROW_TILE, HEAD_DIM), lambda bi, i: (bi, 0, i, 0)),
                  full(pre_g), full(w_u), full(w_c), full(w_b), full(w_ga), full(w_gb),
                  full(gate_b), full(conv_w), full(w_attn), full(w_conv), full(w_out),
                  full(post_g)],
        out_specs=pl.BlockSpec((1, ROW_TILE, d), lambda bi, i: (bi, i, 0)),
        scratch_shapes=[pltpu.VMEM((ROW_TILE + 2 * HALO, d), BF16)],
        compiler_params=pltpu.CompilerParams(
            dimension_semantics=("arbitrary", "arbitrary"),
            vmem_limit_bytes=V7X_VMEM_LIMIT_BYTES),
        name="mixer",
    )(x, x, x, o, pre_g, w_u, w_c, w_b, w_ga, w_gb, gate_b, conv_w, w_attn, w_conv, w_out, post_g)


def _ffn_kernel(x_ref, xp_ref, xn_ref, pre_g_ref, w_up_ref, conv_w_ref, w_d_ref, post_g_ref,
                out_ref, h_scr, hid_scr):
    d_ff = conv_w_ref.shape[1]
    _normed_with_halo(h_scr, x_ref, xp_ref, xn_ref, pre_g_ref)
    h_ext = h_scr[...]
    h = h_scr[pl.ds(HALO, ROW_TILE), :]
    for j in range(d_ff // FF_CHUNK):
        lo, hi = j * FF_CHUNK, (j + 1) * FF_CHUNK
        a_ext = jnp.dot(h_ext, w_up_ref[:, lo:hi], preferred_element_type=F32)
        val = jnp.dot(h, w_up_ref[:, d_ff + lo:d_ff + hi], preferred_element_type=F32)
        gate = _gelu_tanh(_conv3(a_ext, conv_w_ref[:, lo:hi]))
        hid_scr[:, lo:hi] = (gate * val).astype(BF16)
    out = jnp.dot(hid_scr[...], w_d_ref[...], preferred_element_type=F32)
    out_ref[0] = x_ref[0] + out * _rms_scale(out) * post_g_ref[...]


def _ffn(x, pre_g, w_up, conv_w, w_d, post_g):
    b, s, d = x.shape
    main, prev, nxt = _halo_specs(s, d)
    const2 = lambda bi, i: (0, 0)
    once = pl.Buffered(1)
    return pl.pallas_call(
        _ffn_kernel,
        out_shape=jax.ShapeDtypeStruct((b, s, d), F32),
        grid=(b, s // ROW_TILE),
        in_specs=[main, prev, nxt,
                  pl.BlockSpec(pre_g.shape, const2),
                  pl.BlockSpec(w_up.shape, const2, pipeline_mode=once),
                  pl.BlockSpec(conv_w.shape, const2),
                  pl.BlockSpec(w_d.shape, const2, pipeline_mode=once),
                  pl.BlockSpec(post_g.shape, const2)],
        out_specs=pl.BlockSpec((1, ROW_TILE, d), lambda bi, i: (bi, i, 0)),
        scratch_shapes=[
            pltpu.VMEM((ROW_TILE + 2 * HALO, d), BF16),
            pltpu.VMEM((ROW_TILE, w_d.shape[0]), BF16),
        ],
        compiler_params=pltpu.CompilerParams(
            dimension_semantics=("arbitrary", "arbitrary"),
            vmem_limit_bytes=V7X_VMEM_LIMIT_BYTES),
        name="ffn",
    )(x, x, x, pre_g, w_up, conv_w, w_d, post_g)


def _rope_tables(seq):
    t = np.arange(seq)
    freqs = ROPE_THETA ** (-np.arange(ROT_HALF, dtype=np.float64) / ROT_HALF)

    def axis_tables(pos):
        ang = pos.astype(np.float64)[:, None] * freqs[None, :]
        ang = np.concatenate([ang, ang], axis=-1)
        return np.cos(ang), np.sin(ang)

    cos_r, sin_r = axis_tables(t // GRID_W)
    cos_c, sin_c = axis_tables(t % GRID_W)
    sign = np.where(np.arange(AXIS_ROPE_DIM) < ROT_HALF, -1.0, 1.0)
    cos = np.concatenate([cos_r, cos_c], axis=-1).astype(np.float32)
    sin = np.concatenate([sin_r * sign, sin_c * sign], axis=-1).astype(np.float32)
    return cos, sin


def _tile_transposed(table):
    seq = table.shape[0]
    return np.ascontiguousarray(table.reshape(seq // Q_ROWS, Q_ROWS, HEAD_DIM).transpose(0, 2, 1))


def kernel(x, mix_pre_g, w_in, gate_b, q_norm_g, k_norm_g, mix_conv_w, w_attn_proj, w_conv_proj,
           w_out, mix_post_g, ffn_pre_g, w_up, ffn_conv_w, w_down, ffn_post_g):
    depth = w_in.shape[0]
    seq, d = x.shape[1], x.shape[2]
    q_w = N_HEADS * HEAD_DIM
    kv_w = N_KV_HEADS * HEAD_DIM
    d_conv = mix_conv_w.shape[2]
    cos, sin = _rope_tables(seq)
    cos_t, sin_t = _tile_transposed(cos), _tile_transposed(sin)
    for l in range(depth):
        w = w_in[l].astype(BF16)
        o0 = q_w + 2 * kv_w
        w_qkv = w[:, :o0]
        w_u = w[:, o0:o0 + d_conv]
        w_b = w[:, o0 + d_conv:o0 + 2 * d_conv]
        w_c = w[:, o0 + 2 * d_conv:o0 + 3 * d_conv]
        w_ga = w[:, o0 + 3 * d_conv:o0 + 3 * d_conv + d]
        w_gb = w[:, o0 + 3 * d_conv + d:]
        qg_t = jnp.broadcast_to(q_norm_g[l][:, None], (HEAD_DIM, Q_ROWS))
        attn = _attention(x, mix_pre_g[l][None], w_qkv, qg_t, k_norm_g[l][None],
                          cos, sin, cos_t, sin_t)
        x = _mixer(x, attn, mix_pre_g[l][None], w_u, w_c, w_b, w_ga, w_gb, gate_b[l][None],
                   mix_conv_w[l], w_attn_proj[l].astype(BF16), w_conv_proj[l].astype(BF16),
                   w_out[l].astype(BF16), mix_post_g[l][None])
        x = _ffn(x, ffn_pre_g[l][None], w_up[l].astype(BF16), ffn_conv_w[l],
                 w_down[l].astype(BF16), ffn_post_g[l][None])
    return x
```

```python
import numpy as np

import jax
import jax.numpy as jnp
from jax import lax
from jax.experimental import pallas as pl
from jax.experimental.pallas import tpu as pltpu

N_HEADS = 8
N_KV_HEADS = 2
HEAD_DIM = 128
GROUP = N_HEADS // N_KV_HEADS
GRID_W = 64
ROPE_THETA = 10000.0
AXIS_ROPE_DIM = HEAD_DIM // 2
ROT_HALF = AXIS_ROPE_DIM // 2
EPS = 1e-6
SCALE = HEAD_DIM ** -0.5
LOG2_E = 1.4426950408889634

SUBLANES = 8
HALO = 16
ONES_ROWS = 16
V7X_VMEM_LIMIT_BYTES = 56 * 1024 * 1024

ROW_TILE = 1024
Q_ROWS = 512
ITEMS_PER_STEP = 4
EXP2_SAFE_SCORE = 64.0
FF_CHUNK = 256

F32 = jnp.float32
BF16 = jnp.bfloat16


def _rms_scale(v):
    return lax.rsqrt(jnp.mean(v * v, axis=-1, keepdims=True) + EPS)


def _sigmoid(v):
    return 1.0 / (1.0 + jnp.exp(-v))


def _gelu_tanh(v):
    c = 0.7978845608028654
    return 0.5 * v * (1.0 + jnp.tanh(c * (v + 0.044715 * (v * v * v))))


def _attention_kernel(shift_ref, x_hbm, g_ref, w_ref, qg_t_ref, kg_ref, cos_ref, sin_ref,
                      cos_t_ref, sin_t_ref, o_ref, qt_scr, k_scr, vt_scr, s_even, s_odd, m_even,
                      m_odd, x_buf, x_sem):
    seq = x_hbm.shape[1]
    n_tiles = seq // Q_ROWS
    batch = pl.program_id(0)
    n_items = N_HEADS * n_tiles
    tile_bits = n_tiles.bit_length() - 1
    group_bits = GROUP.bit_length() - 1
    q_w = N_HEADS * HEAD_DIM
    kv_w = N_KV_HEADS * HEAD_DIM
    lane = lax.broadcasted_iota(jnp.int32, (Q_ROWS, HEAD_DIM), 1)
    first_half = (lane & ROT_HALF) == 0

    def rope_rows(z, gain, cos, sin):
        zn = z * _rms_scale(z) * gain
        rot = jnp.where(first_half,
                        pltpu.roll(zn, HEAD_DIM - ROT_HALF, 1),
                        pltpu.roll(zn, ROT_HALF, 1))
        return zn * cos + rot * sin

    def rope_cols(zt, gain_t, cos_t, sin_t):
        inv = lax.rsqrt(jnp.mean(zt * zt, axis=0, keepdims=True) + EPS)
        zn = zt * inv * gain_t
        r = ROT_HALF
        rot = jnp.concatenate([zn[r:2 * r], zn[0:r], zn[3 * r:4 * r], zn[2 * r:3 * r]], axis=0)
        return zn * cos_t + rot * sin_t

    def x_copy(bi):
        return pltpu.make_async_copy(x_hbm.at[bi], x_buf, x_sem.at[0])

    def proj_step(r):
        rows = pl.ds(r * Q_ROWS, Q_ROWS)
        xt = x_buf[rows, :]
        h = (xt * _rms_scale(xt) * g_ref[...]).astype(BF16)
        z = jnp.dot(h, w_ref[...], preferred_element_type=F32)
        cos_t = cos_t_ref[r]
        sin_t = sin_t_ref[r]
        for hh in range(N_HEADS):
            zq_t = z[:, hh * HEAD_DIM:(hh + 1) * HEAD_DIM].T
            q_t = rope_cols(zq_t, qg_t_ref[...], cos_t, sin_t) * (SCALE * LOG2_E)
            qt_scr[hh, r] = q_t.astype(BF16)
        cos = cos_ref[rows, :]
        sin = sin_ref[rows, :]
        for kh in range(N_KV_HEADS):
            zk = z[:, q_w + kh * HEAD_DIM:q_w + (kh + 1) * HEAD_DIM]
            k_scr[kh, rows, :] = rope_rows(zk, kg_ref[...], cos, sin).astype(BF16)
            zv = z[:, q_w + kv_w + kh * HEAD_DIM:q_w + kv_w + (kh + 1) * HEAD_DIM]
            vt_scr[kh, r, 0:HEAD_DIM, :] = zv.T.astype(BF16)
            vt_scr[kh, r, HEAD_DIM:, :] = jnp.ones((ONES_ROWS, Q_ROWS), BF16)

    @pl.when(batch == 0)
    def _():
        x_copy(0).start()

    x_copy(batch).wait()
    for r in range(n_tiles):
        proj_step(r)

    def split(item):
        hh = item >> tile_bits
        t = item & (n_tiles - 1)
        kh = hh >> group_bits
        return hh, t, kh

    def scores(item, s_scr, m_scr):
        hh, t, kh = split(item)
        s_t = jnp.dot(k_scr[kh], qt_scr[hh, t], preferred_element_type=F32)
        s_scr[...] = s_t
        m_scr[...] = jnp.max(s_t, axis=0, keepdims=True)

    def finish(item, s_scr, m_scr):
        attend(item, jnp.exp2(s_scr[...] - m_scr[...]).astype(BF16))

    def attend(item, p16):
        hh, t, kh = split(item)
        acc = jnp.dot(vt_scr[kh, 0], p16[0:Q_ROWS], preferred_element_type=F32)
        for kt in range(1, n_tiles):
            acc = acc + jnp.dot(vt_scr[kh, kt], p16[kt * Q_ROWS:(kt + 1) * Q_ROWS],
                                preferred_element_type=F32)
        denom = acc[HEAD_DIM:HEAD_DIM + 1]
        o = (acc[0:HEAD_DIM] * (1.0 / denom)).T
        start = t * Q_ROWS
        if not isinstance(start, int):
            start = pl.multiple_of(start, Q_ROWS)
        o_ref[0, hh, pl.ds(start, Q_ROWS), :] = o.astype(BF16)

    def pair_step(j, carry):
        scores(2 * j + 1, s_odd, m_odd)
        finish(2 * j, s_even, m_even)
        scores(2 * j + 2, s_even, m_even)
        finish(2 * j + 1, s_odd, m_odd)
        return carry

    def unshifted_step(j, carry):
        for i in range(ITEMS_PER_STEP):
            item = ITEMS_PER_STEP * j + i
            hh, t, kh = split(item)
            s_t = jnp.dot(k_scr[kh], qt_scr[hh, t], preferred_element_type=F32)
            attend(item, jnp.exp2(s_t).astype(BF16))
        return carry

    @pl.when(batch + 1 < pl.num_programs(0))
    def _():
        x_copy(batch + 1).start()

    shift = shift_ref[0] != 0

    @pl.when(shift)
    def _():
        scores(0, s_even, m_even)
        lax.fori_loop(0, n_items // 2 - 1, pair_step, 0)
        scores(n_items - 1, s_odd, m_odd)
        finish(n_items - 2, s_even, m_even)
        finish(n_items - 1, s_odd, m_odd)

    @pl.when(jnp.logical_not(shift))
    def _():
        lax.fori_loop(0, n_items // ITEMS_PER_STEP, unshifted_step, 0)


def _attention(shift, x, pre_g, w_qkv, qg_t, k_g, cos, sin, cos_t, sin_t):
    b, s, d = x.shape
    n_tiles = s // Q_ROWS
    const2 = lambda i: (0, 0)
    const3 = lambda i: (0, 0, 0)
    once = pl.Buffered(1)
    return pl.pallas_call(
        _attention_kernel,
        out_shape=jax.ShapeDtypeStruct((b, N_HEADS, s, HEAD_DIM), BF16),
        grid=(b,),
        in_specs=[
            pl.BlockSpec(memory_space=pltpu.SMEM),
            pl.BlockSpec(memory_space=pl.ANY),
            pl.BlockSpec((1, d), const2),
            pl.BlockSpec(w_qkv.shape, const2, pipeline_mode=once),
            pl.BlockSpec(qg_t.shape, const2),
            pl.BlockSpec((1, HEAD_DIM), const2),
            pl.BlockSpec(cos.shape, const2, pipeline_mode=once),
            pl.BlockSpec(sin.shape, const2, pipeline_mode=once),
            pl.BlockSpec(cos_t.shape, const3, pipeline_mode=once),
            pl.BlockSpec(sin_t.shape, const3, pipeline_mode=once),
        ],
        out_specs=pl.BlockSpec((1, N_HEADS, s, HEAD_DIM), lambda i: (i, 0, 0, 0)),
        scratch_shapes=[
            pltpu.VMEM((N_HEADS, n_tiles, HEAD_DIM, Q_ROWS), BF16),
            pltpu.VMEM((N_KV_HEADS, s, HEAD_DIM), BF16),
            pltpu.VMEM((N_KV_HEADS, n_tiles, HEAD_DIM + ONES_ROWS, Q_ROWS), BF16),
            pltpu.VMEM((s, Q_ROWS), F32),
            pltpu.VMEM((s, Q_ROWS), F32),
            pltpu.VMEM((1, Q_ROWS), F32),
            pltpu.VMEM((1, Q_ROWS), F32),
            pltpu.VMEM((s, d), F32),
            pltpu.SemaphoreType.DMA((1,)),
        ],
        compiler_params=pltpu.CompilerParams(
            dimension_semantics=("arbitrary",), vmem_limit_bytes=V7X_VMEM_LIMIT_BYTES),
        name="attention",
    )(shift, x, pre_g, w_qkv, qg_t, k_g, cos, sin, cos_t, sin_t)


def _halo_specs(seq, d):
    per_tile = ROW_TILE // HALO
    last = seq // HALO - 1
    main = pl.BlockSpec((1, ROW_TILE, d), lambda b, i: (b, i, 0))
    prev = pl.BlockSpec((1, HALO, d), lambda b, i: (b, jnp.maximum(i * per_tile - 1, 0), 0))
    nxt = pl.BlockSpec((1, HALO, d), lambda b, i: (b, jnp.minimum((i + 1) * per_tile, last), 0))
    return main, prev, nxt


def _normed_with_halo(h_scr, x_ref, xp_ref, xn_ref, g_ref):
    i = pl.program_id(1)
    gain = g_ref[...]

    def normed(v):
        return v * _rms_scale(v) * gain

    keep_prev = (i > 0).astype(F32)
    keep_next = (i < pl.num_programs(1) - 1).astype(F32)
    h_scr[0:HALO, :] = (normed(xp_ref[0]) * keep_prev).astype(BF16)
    h_scr[HALO:HALO + ROW_TILE, :] = normed(x_ref[0]).astype(BF16)
    h_scr[HALO + ROW_TILE:, :] = (normed(xn_ref[0]) * keep_next).astype(BF16)


def _conv3(ext, w):
    n = ROW_TILE // SUBLANES
    c = ext.shape[1]
    win = ext[HALO - SUBLANES:HALO + ROW_TILE + SUBLANES].reshape(n + 2, SUBLANES, c)
    sub = lax.broadcasted_iota(jnp.int32, (n, SUBLANES, c), 1)
    down = pltpu.roll(win, 1, 1)
    up = pltpu.roll(win, SUBLANES - 1, 1)
    prev = jnp.where(sub == 0, down[0:n], down[1:n + 1])
    nxt = jnp.where(sub == SUBLANES - 1, up[2:n + 2], up[1:n + 1])
    out = prev * w[0:1, :][None] + win[1:n + 1] * w[1:2, :][None] + nxt * w[2:3, :][None]
    return out.reshape(ROW_TILE, c)


def _mixer_kernel(x_ref, xp_ref, xn_ref, o_ref, pre_g_ref, w_u_ref, w_c_ref, w_b_ref,
                  w_ga_ref, w_gb_ref, gate_b_ref, conv_w_ref, w_attn_ref, w_conv_ref,
                  w_out_ref, post_g_ref, out_ref, h_scr):
    d = x_ref.shape[2]
    _normed_with_halo(h_scr, x_ref, xp_ref, xn_ref, pre_g_ref)
    h_ext = h_scr[...]
    u = jnp.dot(h_ext, w_u_ref[...], preferred_element_type=F32)
    c = jnp.dot(h_ext, w_c_ref[...], preferred_element_type=F32)
    h = h_scr[pl.ds(HALO, ROW_TILE), :]
    b_gate = jnp.dot(h, w_b_ref[...], preferred_element_type=F32)
    conv = (b_gate * _conv3(c * u, conv_w_ref[...])).astype(BF16)
    y_b = jnp.dot(conv, w_conv_ref[...], preferred_element_type=F32)
    attn = jnp.concatenate([o_ref[0, hh] for hh in range(N_HEADS)], axis=1)
    y_a = jnp.dot(attn, w_attn_ref[...], preferred_element_type=F32)
    g_a = _sigmoid(jnp.dot(h, w_ga_ref[...], preferred_element_type=F32) + gate_b_ref[:, :d])
    g_b = _sigmoid(jnp.dot(h, w_gb_ref[...], preferred_element_type=F32) + gate_b_ref[:, d:])
    merged = (g_a * y_a + g_b * y_b).astype(BF16)
    out = jnp.dot(merged, w_out_ref[...], preferred_element_type=F32)
    out_ref[0] = x_ref[0] + out * _rms_scale(out) * post_g_ref[...]


def _mixer(x, o, pre_g, w_u, w_c, w_b, w_ga, w_gb, gate_b, conv_w, w_attn, w_conv, w_out, post_g):
    b, s, d = x.shape
    main, prev, nxt = _halo_specs(s, d)
    const = lambda bi, i: (0, 0)
    full = lambda a: pl.BlockSpec(a.shape, const)
    return pl.pallas_call(
        _mixer_kernel,
        out_shape=jax.ShapeDtypeStruct((b, s, d), F32),
        grid=(b, s // ROW_TILE),
        in_specs=[main, prev, nxt,
                  pl.BlockSpec((1, N_HEADS, ROW_TILE, HEAD_DIM), lambda bi, i: (bi, 0, i, 0)),
                  full(pre_g), full(w_u), full(w_c), full(w_b), full(w_ga), full(w_gb),
                  full(gate_b), full(conv_w), full(w_attn), full(w_conv), full(w_out),
                  full(post_g)],
        out_specs=pl.BlockSpec((1, ROW_TILE, d), lambda bi, i: (bi, i, 0)),
        scratch_shapes=[pltpu.VMEM((ROW_TILE + 2 * HALO, d), BF16)],
        compiler_params=pltpu.CompilerParams(
            dimension_semantics=("arbitrary", "arbitrary"),
            vmem_limit_bytes=V7X_VMEM_LIMIT_BYTES),
        name="mixer",
    )(x, x, x, o, pre_g, w_u, w_c, w_b, w_ga, w_gb, gate_b, conv_w, w_attn, w_conv, w_out, post_g)


def _ffn_kernel(x_ref, xp_ref, xn_ref, pre_g_ref, w_up_ref, conv_w_ref, w_d_ref, post_g_ref,
                out_ref, h_scr, hid_scr):
    d_ff = conv_w_ref.shape[1]
    _normed_with_halo(h_scr, x_ref, xp_ref, xn_ref, pre_g_ref)
    h_ext = h_scr[...]
    h = h_scr[pl.ds(HALO, ROW_TILE), :]
    for j in range(d_ff // FF_CHUNK):
        lo, hi = j * FF_CHUNK, (j + 1) * FF_CHUNK
        a_ext = jnp.dot(h_ext, w_up_ref[:, lo:hi], preferred_element_type=F32)
        val = jnp.dot(h, w_up_ref[:, d_ff + lo:d_ff + hi], preferred_element_type=F32)
        gate = _gelu_tanh(_conv3(a_ext, conv_w_ref[:, lo:hi]))
        hid_scr[:, lo:hi] = (gate * val).astype(BF16)
    out = jnp.dot(hid_scr[...], w_d_ref[...], preferred_element_type=F32)
    out_ref[0] = x_ref[0] + out * _rms_scale(out) * post_g_ref[...]


def _ffn(x, pre_g, w_up, conv_w, w_d, post_g):
    b, s, d = x.shape
    main, prev, nxt = _halo_specs(s, d)
    const2 = lambda bi, i: (0, 0)
    once = pl.Buffered(1)
    return pl.pallas_call(
        _ffn_kernel,
        out_shape=jax.ShapeDtypeStruct((b, s, d), F32),
        grid=(b, s // ROW_TILE),
        in_specs=[main, prev, nxt,
                  pl.BlockSpec(pre_g.shape, const2),
                  pl.BlockSpec(w_up.shape, const2, pipeline_mode=once),
                  pl.BlockSpec(conv_w.shape, const2),
                  pl.BlockSpec(w_d.shape, const2, pipeline_mode=once),
                  pl.BlockSpec(post_g.shape, const2)],
        out_specs=pl.BlockSpec((1, ROW_TILE, d), lambda bi, i: (bi, i, 0)),
        scratch_shapes=[
            pltpu.VMEM((ROW_TILE + 2 * HALO, d), BF16),
            pltpu.VMEM((ROW_TILE, w_d.shape[0]), BF16),
        ],
        compiler_params=pltpu.CompilerParams(
            dimension_semantics=("arbitrary", "arbitrary"),
            vmem_limit_bytes=V7X_VMEM_LIMIT_BYTES),
        name="ffn",
    )(x, x, x, pre_g, w_up, conv_w, w_d, post_g)


def _rope_tables(seq):
    t = np.arange(seq)
    freqs = ROPE_THETA ** (-np.arange(ROT_HALF, dtype=np.float64) / ROT_HALF)

    def axis_tables(pos):
        ang = pos.astype(np.float64)[:, None] * freqs[None, :]
        ang = np.concatenate([ang, ang], axis=-1)
        return np.cos(ang), np.sin(ang)

    cos_r, sin_r = axis_tables(t // GRID_W)
    cos_c, sin_c = axis_tables(t % GRID_W)
    sign = np.where(np.arange(AXIS_ROPE_DIM) < ROT_HALF, -1.0, 1.0)
    cos = np.concatenate([cos_r, cos_c], axis=-1).astype(np.float32)
    sin = np.concatenate([sin_r * sign, sin_c * sign], axis=-1).astype(np.float32)
    return cos, sin


def _tile_transposed(table):
    seq = table.shape[0]
    return np.ascontiguousarray(table.reshape(seq // Q_ROWS, Q_ROWS, HEAD_DIM).transpose(0, 2, 1))


def kernel(x, mix_pre_g, w_in, gate_b, q_norm_g, k_norm_g, mix_conv_w, w_attn_proj, w_conv_proj,
           w_out, mix_post_g, ffn_pre_g, w_up, ffn_conv_w, w_down, ffn_post_g):
    depth = w_in.shape[0]
    seq, d = x.shape[1], x.shape[2]
    q_w = N_HEADS * HEAD_DIM
    kv_w = N_KV_HEADS * HEAD_DIM
    d_conv = mix_conv_w.shape[2]
    cos, sin = _rope_tables(seq)
    cos_t, sin_t = _tile_transposed(cos), _tile_transposed(sin)
    for l in range(depth):
        w = w_in[l].astype(BF16)
        o0 = q_w + 2 * kv_w
        w_qkv = w[:, :o0]
        w_u = w[:, o0:o0 + d_conv]
        w_b = w[:, o0 + d_conv:o0 + 2 * d_conv]
        w_c = w[:, o0 + 2 * d_conv:o0 + 3 * d_conv]
        w_ga = w[:, o0 + 3 * d_conv:o0 + 3 * d_conv + d]
        w_gb = w[:, o0 + 3 * d_conv + d:]
        qg_t = jnp.broadcast_to(q_norm_g[l][:, None], (HEAD_DIM, Q_ROWS))
        score_bound = (HEAD_DIM * SCALE * LOG2_E * 1.01
                       * jnp.max(jnp.abs(q_norm_g[l])) * jnp.max(jnp.abs(k_norm_g[l])))
        shift = jnp.logical_not(score_bound <= EXP2_SAFE_SCORE).astype(jnp.int32).reshape(1)
        attn = _attention(shift, x, mix_pre_g[l][None], w_qkv, qg_t, k_norm_g[l][None],
                          cos, sin, cos_t, sin_t)
        x = _mixer(x, attn, mix_pre_g[l][None], w_u, w_c, w_b, w_ga, w_gb, gate_b[l][None],
                   mix_conv_w[l], w_attn_proj[l].astype(BF16), w_conv_proj[l].astype(BF16),
                   w_out[l].astype(BF16), mix_post_g[l][None])
        x = _ffn(x, ffn_pre_g[l][None], w_up[l].astype(BF16), ffn_conv_w[l],
                 w_down[l].astype(BF16), ffn_post_g[l][None])
    return x
```

```python
import numpy as np

import jax
import jax.numpy as jnp
from jax import lax
from jax.experimental import pallas as pl
from jax.experimental.pallas import tpu as pltpu

N_HEADS = 8
N_KV_HEADS = 2
HEAD_DIM = 128
GROUP = N_HEADS // N_KV_HEADS
GRID_W = 64
ROPE_THETA = 10000.0
AXIS_ROPE_DIM = HEAD_DIM // 2
ROT_HALF = AXIS_ROPE_DIM // 2
EPS = 1e-6
SCALE = HEAD_DIM ** -0.5
LOG2_E = 1.4426950408889634

SUBLANES = 8
HALO = 16
ONES_ROWS = 16
V7X_VMEM_LIMIT_BYTES = 56 * 1024 * 1024

ROW_TILE = 1024
Q_ROWS = 512
ITEMS_PER_STEP = 8
EXP2_SAFE_SCORE = 64.0
FF_CHUNK = 256

F32 = jnp.float32
BF16 = jnp.bfloat16


def _rms_scale(v):
    return lax.rsqrt(jnp.mean(v * v, axis=-1, keepdims=True) + EPS)


def _sigmoid(v):
    return 1.0 / (1.0 + jnp.exp(-v))


def _gelu_tanh(v):
    c = 0.7978845608028654
    return 0.5 * v * (1.0 + jnp.tanh(c * (v + 0.044715 * (v * v * v))))


def _attention_kernel(shift_ref, x_hbm, g_ref, w_ref, qg_t_ref, kg_ref, cos_ref, sin_ref,
                      cos_t_ref, sin_t_ref, o_ref, qt_scr, k_scr, vt_scr, s_even, s_odd, m_even,
                      m_odd, x_buf, x_sem):
    seq = x_hbm.shape[1]
    n_tiles = seq // Q_ROWS
    batch = pl.program_id(0)
    n_items = N_HEADS * n_tiles
    tile_bits = n_tiles.bit_length() - 1
    group_bits = GROUP.bit_length() - 1
    q_w = N_HEADS * HEAD_DIM
    kv_w = N_KV_HEADS * HEAD_DIM
    lane = lax.broadcasted_iota(jnp.int32, (Q_ROWS, HEAD_DIM), 1)
    first_half = (lane & ROT_HALF) == 0

    def rope_rows(z, gain, cos, sin):
        zn = z * _rms_scale(z) * gain
        rot = jnp.where(first_half,
                        pltpu.roll(zn, HEAD_DIM - ROT_HALF, 1),
                        pltpu.roll(zn, ROT_HALF, 1))
        return zn * cos + rot * sin

    def rope_cols(zt, gain_t, cos_t, sin_t):
        inv = lax.rsqrt(jnp.mean(zt * zt, axis=0, keepdims=True) + EPS)
        zn = zt * inv * gain_t
        r = ROT_HALF
        rot = jnp.concatenate([zn[r:2 * r], zn[0:r], zn[3 * r:4 * r], zn[2 * r:3 * r]], axis=0)
        return zn * cos_t + rot * sin_t

    def x_copy(bi):
        return pltpu.make_async_copy(x_hbm.at[bi], x_buf, x_sem.at[0])

    def proj_step(r):
        rows = pl.ds(r * Q_ROWS, Q_ROWS)
        xt = x_buf[rows, :]
        h = (xt * _rms_scale(xt) * g_ref[...]).astype(BF16)
        z = jnp.dot(h, w_ref[...], preferred_element_type=F32)
        cos_t = cos_t_ref[r]
        sin_t = sin_t_ref[r]
        for hh in range(N_HEADS):
            zq_t = z[:, hh * HEAD_DIM:(hh + 1) * HEAD_DIM].T
            q_t = rope_cols(zq_t, qg_t_ref[...], cos_t, sin_t) * (SCALE * LOG2_E)
            qt_scr[hh, r] = q_t.astype(BF16)
        cos = cos_ref[rows, :]
        sin = sin_ref[rows, :]
        for kh in range(N_KV_HEADS):
            zk = z[:, q_w + kh * HEAD_DIM:q_w + (kh + 1) * HEAD_DIM]
            k_scr[kh, rows, :] = rope_rows(zk, kg_ref[...], cos, sin).astype(BF16)
            zv = z[:, q_w + kv_w + kh * HEAD_DIM:q_w + kv_w + (kh + 1) * HEAD_DIM]
            vt_scr[kh, r, 0:HEAD_DIM, :] = zv.T.astype(BF16)
            vt_scr[kh, r, HEAD_DIM:, :] = jnp.ones((ONES_ROWS, Q_ROWS), BF16)

    @pl.when(batch == 0)
    def _():
        x_copy(0).start()

    x_copy(batch).wait()
    for r in range(n_tiles):
        proj_step(r)

    def split(item):
        hh = item >> tile_bits
        t = item & (n_tiles - 1)
        kh = hh >> group_bits
        return hh, t, kh

    def tile_rows(t):
        start = t * Q_ROWS
        if not isinstance(start, int):
            start = pl.multiple_of(start, Q_ROWS)
        return pl.ds(start, Q_ROWS)

    def scores(item, s_scr, m_scr):
        hh, t, kh = split(item)
        s_t = jnp.dot(k_scr[kh], qt_scr[hh, t], preferred_element_type=F32)
        s_scr[...] = s_t
        m_scr[...] = jnp.max(s_t, axis=0, keepdims=True)

    def finish(item, s_scr, m_scr):
        attend(item, jnp.exp2(s_scr[...] - m_scr[...]).astype(BF16))

    def attend(item, p16, denom=None):
        hh, t, kh = split(item)
        v_rows = HEAD_DIM + (ONES_ROWS if denom is None else 0)
        acc = jnp.dot(vt_scr[kh, 0, 0:v_rows, :], p16[0:Q_ROWS], preferred_element_type=F32)
        for kt in range(1, n_tiles):
            acc = acc + jnp.dot(vt_scr[kh, kt, 0:v_rows, :], p16[kt * Q_ROWS:(kt + 1) * Q_ROWS],
                                preferred_element_type=F32)
        if denom is None:
            denom = acc[HEAD_DIM:HEAD_DIM + 1]
        o = (acc[0:HEAD_DIM] * (1.0 / denom)).T
        o_ref[0, hh, tile_rows(t), :] = o.astype(BF16)

    def pair_step(j, carry):
        scores(2 * j + 1, s_odd, m_odd)
        finish(2 * j, s_even, m_even)
        scores(2 * j + 2, s_even, m_even)
        finish(2 * j + 1, s_odd, m_odd)
        return carry

    def unshifted_step(j, carry):
        for i in range(ITEMS_PER_STEP):
            item = ITEMS_PER_STEP * j + i
            hh, t, kh = split(item)
            p_t = jnp.exp2(jnp.dot(k_scr[kh], qt_scr[hh, t], preferred_element_type=F32))
            attend(item, p_t.astype(BF16), jnp.sum(p_t, axis=0, keepdims=True))
        return carry

    @pl.when(batch + 1 < pl.num_programs(0))
    def _():
        x_copy(batch + 1).start()

    shift = shift_ref[0] != 0

    @pl.when(shift)
    def _():
        scores(0, s_even, m_even)
        lax.fori_loop(0, n_items // 2 - 1, pair_step, 0)
        scores(n_items - 1, s_odd, m_odd)
        finish(n_items - 2, s_even, m_even)
        finish(n_items - 1, s_odd, m_odd)

    @pl.when(jnp.logical_not(shift))
    def _():
        lax.fori_loop(0, n_items // ITEMS_PER_STEP, unshifted_step, 0)


def _attention(shift, x, pre_g, w_qkv, qg_t, k_g, cos, sin, cos_t, sin_t):
    b, s, d = x.shape
    n_tiles = s // Q_ROWS
    const2 = lambda i: (0, 0)
    const3 = lambda i: (0, 0, 0)
    once = pl.Buffered(1)
    return pl.pallas_call(
        _attention_kernel,
        out_shape=jax.ShapeDtypeStruct((b, N_HEADS, s, HEAD_DIM), BF16),
        grid=(b,),
        in_specs=[
            pl.BlockSpec(memory_space=pltpu.SMEM),
            pl.BlockSpec(memory_space=pl.ANY),
            pl.BlockSpec((1, d), const2),
            pl.BlockSpec(w_qkv.shape, const2, pipeline_mode=once),
            pl.BlockSpec(qg_t.shape, const2),
            pl.BlockSpec((1, HEAD_DIM), const2),
            pl.BlockSpec(cos.shape, const2, pipeline_mode=once),
            pl.BlockSpec(sin.shape, const2, pipeline_mode=once),
            pl.BlockSpec(cos_t.shape, const3, pipeline_mode=once),
            pl.BlockSpec(sin_t.shape, const3, pipeline_mode=once),
        ],
        out_specs=pl.BlockSpec((1, N_HEADS, s, HEAD_DIM), lambda i: (i, 0, 0, 0)),
        scratch_shapes=[
            pltpu.VMEM((N_HEADS, n_tiles, HEAD_DIM, Q_ROWS), BF16),
            pltpu.VMEM((N_KV_HEADS, s, HEAD_DIM), BF16),
            pltpu.VMEM((N_KV_HEADS, n_tiles, HEAD_DIM + ONES_ROWS, Q_ROWS), BF16),
            pltpu.VMEM((s, Q_ROWS), F32),
            pltpu.VMEM((s, Q_ROWS), F32),
            pltpu.VMEM((1, Q_ROWS), F32),
            pltpu.VMEM((1, Q_ROWS), F32),
            pltpu.VMEM((s, d), F32),
            pltpu.SemaphoreType.DMA((1,)),
        ],
        compiler_params=pltpu.CompilerParams(
            dimension_semantics=("arbitrary",), vmem_limit_bytes=V7X_VMEM_LIMIT_BYTES),
        name="attention",
    )(shift, x, pre_g, w_qkv, qg_t, k_g, cos, sin, cos_t, sin_t)


def _halo_specs(seq, d):
    per_tile = ROW_TILE // HALO
    last = seq // HALO - 1
    main = pl.BlockSpec((1, ROW_TILE, d), lambda b, i: (b, i, 0))
    prev = pl.BlockSpec((1, HALO, d), lambda b, i: (b, jnp.maximum(i * per_tile - 1, 0), 0))
    nxt = pl.BlockSpec((1, HALO, d), lambda b, i: (b, jnp.minimum((i + 1) * per_tile, last), 0))
    return main, prev, nxt


def _normed_with_halo(h_scr, x_ref, xp_ref, xn_ref, g_ref):
    i = pl.program_id(1)
    gain = g_ref[...]

    def normed(v):
        return v * _rms_scale(v) * gain

    keep_prev = (i > 0).astype(F32)
    keep_next = (i < pl.num_programs(1) - 1).astype(F32)
    h_scr[0:HALO, :] = (normed(xp_ref[0]) * keep_prev).astype(BF16)
    h_scr[HALO:HALO + ROW_TILE, :] = normed(x_ref[0]).astype(BF16)
    h_scr[HALO + ROW_TILE:, :] = (normed(xn_ref[0]) * keep_next).astype(BF16)


def _conv3(ext, w):
    n = ROW_TILE // SUBLANES
    c = ext.shape[1]
    win = ext[HALO - SUBLANES:HALO + ROW_TILE + SUBLANES].reshape(n + 2, SUBLANES, c)
    sub = lax.broadcasted_iota(jnp.int32, (n, SUBLANES, c), 1)
    down = pltpu.roll(win, 1, 1)
    up = pltpu.roll(win, SUBLANES - 1, 1)
    prev = jnp.where(sub == 0, down[0:n], down[1:n + 1])
    nxt = jnp.where(sub == SUBLANES - 1, up[2:n + 2], up[1:n + 1])
    out = prev * w[0:1, :][None] + win[1:n + 1] * w[1:2, :][None] + nxt * w[2:3, :][None]
    return out.reshape(ROW_TILE, c)


def _mixer_kernel(x_ref, xp_ref, xn_ref, o_ref, pre_g_ref, w_u_ref, w_c_ref, w_b_ref,
                  w_ga_ref, w_gb_ref, gate_b_ref, conv_w_ref, w_attn_ref, w_conv_ref,
                  w_out_ref, post_g_ref, out_ref, h_scr):
    d = x_ref.shape[2]
    _normed_with_halo(h_scr, x_ref, xp_ref, xn_ref, pre_g_ref)
    h_ext = h_scr[...]
    u = jnp.dot(h_ext, w_u_ref[...], preferred_element_type=F32)
    c = jnp.dot(h_ext, w_c_ref[...], preferred_element_type=F32)
    h = h_scr[pl.ds(HALO, ROW_TILE), :]
    b_gate = jnp.dot(h, w_b_ref[...], preferred_element_type=F32)
    conv = (b_gate * _conv3(c * u, conv_w_ref[...])).astype(BF16)
    y_b = jnp.dot(conv, w_conv_ref[...], preferred_element_type=F32)
    attn = jnp.concatenate([o_ref[0, hh] for hh in range(N_HEADS)], axis=1)
    y_a = jnp.dot(attn, w_attn_ref[...], preferred_element_type=F32)
    g_a = _sigmoid(jnp.dot(h, w_ga_ref[...], preferred_element_type=F32) + gate_b_ref[:, :d])
    g_b = _sigmoid(jnp.dot(h, w_gb_ref[...], preferred_element_type=F32) + gate_b_ref[:, d:])
    merged = (g_a * y_a + g_b * y_b).astype(BF16)
    out = jnp.dot(merged, w_out_ref[...], preferred_element_type=F32)
    out_ref[0] = x_ref[0] + out * _rms_scale(out) * post_g_ref[...]


def _mixer(x, o, pre_g, w_u, w_c, w_b, w_ga, w_gb, gate_b, conv_w, w_attn, w_conv, w_out, post_g):
    b, s, d = x.shape
    main, prev, nxt = _halo_specs(s, d)
    const = lambda bi, i: (0, 0)
    full = lambda a: pl.BlockSpec(a.shape, const)
    return pl.pallas_call(
        _mixer_kernel,
        out_shape=jax.ShapeDtypeStruct((b, s, d), F32),
        grid=(b, s // ROW_TILE),
        in_specs=[main, prev, nxt,
                  pl.BlockSpec((1, N_HEADS, ROW_TILE, HEAD_DIM), lambda bi, i: (bi, 0, i, 0)),
                  full(pre_g), full(w_u), full(w_c), full(w_b), full(w_ga), full(w_gb),
                  full(gate_b), full(conv_w), full(w_attn), full(w_conv), full(w_out),
                  full(post_g)],
        out_specs=pl.BlockSpec((1, ROW_TILE, d), lambda bi, i: (bi, i, 0)),
        scratch_shapes=[pltpu.VMEM((ROW_TILE + 2 * HALO, d), BF16)],
        compiler_params=pltpu.CompilerParams(
            dimension_semantics=("arbitrary", "arbitrary"),
            vmem_limit_bytes=V7X_VMEM_LIMIT_BYTES),
        name="mixer",
    )(x, x, x, o, pre_g, w_u, w_c, w_b, w_ga, w_gb, gate_b, conv_w, w_attn, w_conv, w_out, post_g)


def _ffn_kernel(x_ref, xp_ref, xn_ref, pre_g_ref, w_up_ref, conv_w_ref, w_d_ref, post_g_ref,
                out_ref, h_scr, hid_scr):
    d_ff = conv_w_ref.shape[1]
    _normed_with_halo(h_scr, x_ref, xp_ref, xn_ref, pre_g_ref)
    h_ext = h_scr[...]
    h = h_scr[pl.ds(HALO, ROW_TILE), :]
    for j in range(d_ff // FF_CHUNK):
        lo, hi = j * FF_CHUNK, (j + 1) * FF_CHUNK
        a_ext = jnp.dot(h_ext, w_up_ref[:, lo:hi], preferred_element_type=F32)
        val = jnp.dot(h, w_up_ref[:, d_ff + lo:d_ff + hi], preferred_element_type=F32)
        gate = _gelu_tanh(_conv3(a_ext, conv_w_ref[:, lo:hi]))
        hid_scr[:, lo:hi] = (gate * val).astype(BF16)
    out = jnp.dot(hid_scr[...], w_d_ref[...], preferred_element_type=F32)
    out_ref[0] = x_ref[0] + out * _rms_scale(out) * post_g_ref[...]


def _ffn(x, pre_g, w_up, conv_w, w_d, post_g):
    b, s, d = x.shape
    main, prev, nxt = _halo_specs(s, d)
    const2 = lambda bi, i: (0, 0)
    once = pl.Buffered(1)
    return pl.pallas_call(
        _ffn_kernel,
        out_shape=jax.ShapeDtypeStruct((b, s, d), F32),
        grid=(b, s // ROW_TILE),
        in_specs=[main, prev, nxt,
                  pl.BlockSpec(pre_g.shape, const2),
                  pl.BlockSpec(w_up.shape, const2, pipeline_mode=once),
                  pl.BlockSpec(conv_w.shape, const2),
                  pl.BlockSpec(w_d.shape, const2, pipeline_mode=once),
                  pl.BlockSpec(post_g.shape, const2)],
        out_specs=pl.BlockSpec((1, ROW_TILE, d), lambda bi, i: (bi, i, 0)),
        scratch_shapes=[
            pltpu.VMEM((ROW_TILE + 2 * HALO, d), BF16),
            pltpu.VMEM((ROW_TILE, w_d.shape[0]), BF16),
        ],
        compiler_params=pltpu.CompilerParams(
            dimension_semantics=("arbitrary", "arbitrary"),
            vmem_limit_bytes=V7X_VMEM_LIMIT_BYTES),
        name="ffn",
    )(x, x, x, pre_g, w_up, conv_w, w_d, post_g)


def _rope_tables(seq):
    t = np.arange(seq)
    freqs = ROPE_THETA ** (-np.arange(ROT_HALF, dtype=np.float64) / ROT_HALF)

    def axis_tables(pos):
        ang = pos.astype(np.float64)[:, None] * freqs[None, :]
        ang = np.concatenate([ang, ang], axis=-1)
        return np.cos(ang), np.sin(ang)

    cos_r, sin_r = axis_tables(t // GRID_W)
    cos_c, sin_c = axis_tables(t % GRID_W)
    sign = np.where(np.arange(AXIS_ROPE_DIM) < ROT_HALF, -1.0, 1.0)
    cos = np.concatenate([cos_r, cos_c], axis=-1).astype(np.float32)
    sin = np.concatenate([sin_r * sign, sin_c * sign], axis=-1).astype(np.float32)
    return cos, sin


def _tile_transposed(table):
    seq = table.shape[0]
    return np.ascontiguousarray(table.reshape(seq // Q_ROWS, Q_ROWS, HEAD_DIM).transpose(0, 2, 1))


def kernel(x, mix_pre_g, w_in, gate_b, q_norm_g, k_norm_g, mix_conv_w, w_attn_proj, w_conv_proj,
           w_out, mix_post_g, ffn_pre_g, w_up, ffn_conv_w, w_down, ffn_post_g):
    depth = w_in.shape[0]
    seq, d = x.shape[1], x.shape[2]
    q_w = N_HEADS * HEAD_DIM
    kv_w = N_KV_HEADS * HEAD_DIM
    d_conv = mix_conv_w.shape[2]
    cos, sin = _rope_tables(seq)
    cos_t, sin_t = _tile_transposed(cos), _tile_transposed(sin)
    for l in range(depth):
        w = w_in[l].astype(BF16)
        o0 = q_w + 2 * kv_w
        w_qkv = w[:, :o0]
        w_u = w[:, o0:o0 + d_conv]
        w_b = w[:, o0 + d_conv:o0 + 2 * d_conv]
        w_c = w[:, o0 + 2 * d_conv:o0 + 3 * d_conv]
        w_ga = w[:, o0 + 3 * d_conv:o0 + 3 * d_conv + d]
        w_gb = w[:, o0 + 3 * d_conv + d:]
        qg_t = jnp.broadcast_to(q_norm_g[l][:, None], (HEAD_DIM, Q_ROWS))
        score_bound = (HEAD_DIM * SCALE * LOG2_E * 1.01
                       * jnp.max(jnp.abs(q_norm_g[l])) * jnp.max(jnp.abs(k_norm_g[l])))
        shift = jnp.logical_not(score_bound <= EXP2_SAFE_SCORE).astype(jnp.int32).reshape(1)
        attn = _attention(shift, x, mix_pre_g[l][None], w_qkv, qg_t, k_norm_g[l][None],
                          cos, sin, cos_t, sin_t)
        x = _mixer(x, attn, mix_pre_g[l][None], w_u, w_c, w_b, w_ga, w_gb, gate_b[l][None],
                   mix_conv_w[l], w_attn_proj[l].astype(BF16), w_conv_proj[l].astype(BF16),
                   w_out[l].astype(BF16), mix_post_g[l][None])
        x = _ffn(x, ffn_pre_g[l][None], w_up[l].astype(BF16), ffn_conv_w[l],
                 w_down[l].astype(BF16), ffn_post_g[l][None])
    return x
```

```python
import numpy as np

import jax
import jax.numpy as jnp
from jax import lax
from jax.experimental import pallas as pl
from jax.experimental.pallas import tpu as pltpu

N_HEADS = 8
N_KV_HEADS = 2
HEAD_DIM = 128
GROUP = N_HEADS // N_KV_HEADS
MIXER_IN_BLOCKS = 5
GRID_W = 64
ROPE_THETA = 10000.0
AXIS_ROPE_DIM = HEAD_DIM // 2
ROT_HALF = AXIS_ROPE_DIM // 2
EPS = 1e-6
SCALE = HEAD_DIM ** -0.5
LOG2_E = 1.4426950408889634

SUBLANES = 8
HALO = 16
ONES_ROWS = 16
V7X_VMEM_LIMIT_BYTES = 56 * 1024 * 1024
WEIGHT_CHUNK_BYTES = 1024 * 1024

ROW_TILE = 1024
Q_ROWS = 512
ITEMS_PER_STEP = 8
EXP2_SAFE_SCORE = 64.0
FF_CHUNK = 256

F32 = jnp.float32
BF16 = jnp.bfloat16


def _rms_scale(v):
    return lax.rsqrt(jnp.mean(v * v, axis=-1, keepdims=True) + EPS)


def _sigmoid(v):
    return 1.0 / (1.0 + jnp.exp(-v))


def _gelu_tanh(v):
    c = 0.7978845608028654
    return 0.5 * v * (1.0 + jnp.tanh(c * (v + 0.044715 * (v * v * v))))


def _attention_kernel(shift_ref, x_hbm, g_ref, w_in_hbm, qg_t_ref, kg_ref, cos_ref, sin_ref,
                      cos_t_ref, sin_t_ref, o_ref, qt_scr, k_scr, vt_scr, s_even, s_odd, m_even,
                      m_odd, x_buf, x_sem, w_ref, w_stage, w_sem):
    seq = x_hbm.shape[1]
    n_tiles = seq // Q_ROWS
    batch = pl.program_id(0)
    n_items = N_HEADS * n_tiles
    tile_bits = n_tiles.bit_length() - 1
    group_bits = GROUP.bit_length() - 1
    q_w = N_HEADS * HEAD_DIM
    kv_w = N_KV_HEADS * HEAD_DIM
    lane = lax.broadcasted_iota(jnp.int32, (Q_ROWS, HEAD_DIM), 1)
    first_half = (lane & ROT_HALF) == 0

    def rope_rows(z, gain, cos, sin):
        zn = z * _rms_scale(z) * gain
        rot = jnp.where(first_half,
                        pltpu.roll(zn, HEAD_DIM - ROT_HALF, 1),
                        pltpu.roll(zn, ROT_HALF, 1))
        return zn * cos + rot * sin

    def rope_cols(zt, gain_t, cos_t, sin_t):
        inv = lax.rsqrt(jnp.mean(zt * zt, axis=0, keepdims=True) + EPS)
        zn = zt * inv * gain_t
        r = ROT_HALF
        rot = jnp.concatenate([zn[r:2 * r], zn[0:r], zn[3 * r:4 * r], zn[2 * r:3 * r]], axis=0)
        return zn * cos_t + rot * sin_t

    def x_copy(bi):
        return pltpu.make_async_copy(x_hbm.at[bi], x_buf, x_sem.at[0])

    def proj_step(r):
        rows = pl.ds(r * Q_ROWS, Q_ROWS)
        xt = x_buf[rows, :]
        h = (xt * _rms_scale(xt) * g_ref[...]).astype(BF16)
        z = jnp.dot(h, w_ref[...], preferred_element_type=F32)
        cos_t = cos_t_ref[r]
        sin_t = sin_t_ref[r]
        for hh in range(N_HEADS):
            zq_t = z[:, hh * HEAD_DIM:(hh + 1) * HEAD_DIM].T
            q_t = rope_cols(zq_t, qg_t_ref[...], cos_t, sin_t) * (SCALE * LOG2_E)
            qt_scr[hh, r] = q_t.astype(BF16)
        cos = cos_ref[rows, :]
        sin = sin_ref[rows, :]
        for kh in range(N_KV_HEADS):
            zk = z[:, q_w + kh * HEAD_DIM:q_w + (kh + 1) * HEAD_DIM]
            k_scr[kh, rows, :] = rope_rows(zk, kg_ref[...], cos, sin).astype(BF16)
            zv = z[:, q_w + kv_w + kh * HEAD_DIM:q_w + kv_w + (kh + 1) * HEAD_DIM]
            vt_scr[kh, r, 0:HEAD_DIM, :] = zv.T.astype(BF16)
            vt_scr[kh, r, HEAD_DIM:, :] = jnp.ones((ONES_ROWS, Q_ROWS), BF16)

    @pl.when(batch == 0)
    def _():
        x_copy(0).start()
        _load_weight_bf16(w_in_hbm.at[:, pl.ds(0, w_ref.shape[1])], w_ref, w_stage, w_sem)

    x_copy(batch).wait()
    for r in range(n_tiles):
        proj_step(r)

    def split(item):
        hh = item >> tile_bits
        t = item & (n_tiles - 1)
        kh = hh >> group_bits
        return hh, t, kh

    def tile_rows(t):
        start = t * Q_ROWS
        if not isinstance(start, int):
            start = pl.multiple_of(start, Q_ROWS)
        return pl.ds(start, Q_ROWS)

    def scores(item, s_scr, m_scr):
        hh, t, kh = split(item)
        s_t = jnp.dot(k_scr[kh], qt_scr[hh, t], preferred_element_type=F32)
        s_scr[...] = s_t
        m_scr[...] = jnp.max(s_t, axis=0, keepdims=True)

    def finish(item, s_scr, m_scr):
        attend(item, jnp.exp2(s_scr[...] - m_scr[...]).astype(BF16))

    def attend(item, p16, denom=None):
        hh, t, kh = split(item)
        v_rows = HEAD_DIM + (ONES_ROWS if denom is None else 0)
        acc = jnp.dot(vt_scr[kh, 0, 0:v_rows, :], p16[0:Q_ROWS], preferred_element_type=F32)
        for kt in range(1, n_tiles):
            acc = acc + jnp.dot(vt_scr[kh, kt, 0:v_rows, :], p16[kt * Q_ROWS:(kt + 1) * Q_ROWS],
                                preferred_element_type=F32)
        if denom is None:
            denom = acc[HEAD_DIM:HEAD_DIM + 1]
        o = (acc[0:HEAD_DIM] * (1.0 / denom)).T
        o_ref[0, hh, tile_rows(t), :] = o.astype(BF16)

    def pair_step(j, carry):
        scores(2 * j + 1, s_odd, m_odd)
        finish(2 * j, s_even, m_even)
        scores(2 * j + 2, s_even, m_even)
        finish(2 * j + 1, s_odd, m_odd)
        return carry

    def unshifted_step(j, carry):
        for i in range(ITEMS_PER_STEP):
            item = ITEMS_PER_STEP * j + i
            hh, t, kh = split(item)
            p_t = jnp.exp2(jnp.dot(k_scr[kh], qt_scr[hh, t], preferred_element_type=F32))
            attend(item, p_t.astype(BF16), jnp.sum(p_t, axis=0, keepdims=True))
        return carry

    @pl.when(batch + 1 < pl.num_programs(0))
    def _():
        x_copy(batch + 1).start()

    shift = shift_ref[0] != 0

    @pl.when(shift)
    def _():
        scores(0, s_even, m_even)
        lax.fori_loop(0, n_items // 2 - 1, pair_step, 0)
        scores(n_items - 1, s_odd, m_odd)
        finish(n_items - 2, s_even, m_even)
        finish(n_items - 1, s_odd, m_odd)

    @pl.when(jnp.logical_not(shift))
    def _():
        lax.fori_loop(0, n_items // ITEMS_PER_STEP, unshifted_step, 0)


def _attention(shift, x, pre_g, w_in, qg_t, k_g, cos, sin, cos_t, sin_t):
    b, s, d = x.shape
    n_tiles = s // Q_ROWS
    qkv_w = (N_HEADS + 2 * N_KV_HEADS) * HEAD_DIM
    const2 = lambda i: (0, 0)
    const3 = lambda i: (0, 0, 0)
    once = pl.Buffered(1)
    return pl.pallas_call(
        _attention_kernel,
        out_shape=jax.ShapeDtypeStruct((b, N_HEADS, s, HEAD_DIM), BF16),
        grid=(b,),
        in_specs=[
            pl.BlockSpec(memory_space=pltpu.SMEM),
            pl.BlockSpec(memory_space=pl.ANY),
            pl.BlockSpec((1, d), const2),
            pl.BlockSpec(memory_space=pl.ANY),
            pl.BlockSpec(qg_t.shape, const2),
            pl.BlockSpec((1, HEAD_DIM), const2),
            pl.BlockSpec(cos.shape, const2, pipeline_mode=once),
            pl.BlockSpec(sin.shape, const2, pipeline_mode=once),
            pl.BlockSpec(cos_t.shape, const3, pipeline_mode=once),
            pl.BlockSpec(sin_t.shape, const3, pipeline_mode=once),
        ],
        out_specs=pl.BlockSpec((1, N_HEADS, s, HEAD_DIM), lambda i: (i, 0, 0, 0)),
        scratch_shapes=[
            pltpu.VMEM((N_HEADS, n_tiles, HEAD_DIM, Q_ROWS), BF16),
            pltpu.VMEM((N_KV_HEADS, s, HEAD_DIM), BF16),
            pltpu.VMEM((N_KV_HEADS, n_tiles, HEAD_DIM + ONES_ROWS, Q_ROWS), BF16),
            pltpu.VMEM((s, Q_ROWS), F32),
            pltpu.VMEM((s, Q_ROWS), F32),
            pltpu.VMEM((1, Q_ROWS), F32),
            pltpu.VMEM((1, Q_ROWS), F32),
            pltpu.VMEM((s, d), F32),
            pltpu.SemaphoreType.DMA((1,)),
            pltpu.VMEM((d, qkv_w), BF16),
            _weight_stage(d, qkv_w),
            pltpu.SemaphoreType.DMA((2,)),
        ],
        compiler_params=pltpu.CompilerParams(
            dimension_semantics=("arbitrary",), vmem_limit_bytes=V7X_VMEM_LIMIT_BYTES),
        name="attention",
    )(shift, x, pre_g, w_in, qg_t, k_g, cos, sin, cos_t, sin_t)


def _halo_specs(seq, d):
    per_tile = ROW_TILE // HALO
    last = seq // HALO - 1
    main = pl.BlockSpec((1, ROW_TILE, d), lambda b, i: (b, i, 0))
    prev = pl.BlockSpec((1, HALO, d), lambda b, i: (b, jnp.maximum(i * per_tile - 1, 0), 0))
    nxt = pl.BlockSpec((1, HALO, d), lambda b, i: (b, jnp.minimum((i + 1) * per_tile, last), 0))
    return main, prev, nxt


def _weight_stage(rows, cols):
    chunk = 1 << ((WEIGHT_CHUNK_BYTES // (4 * cols)).bit_length() - 1)
    while rows % chunk:
        chunk //= 2
    assert chunk >= SUBLANES
    return pltpu.VMEM((2, chunk, cols), F32)


def _load_weight_bf16(src, dst, stage, sem):
    chunk_rows = stage.shape[1]
    n_chunks, rem = divmod(dst.shape[0], chunk_rows)
    assert rem == 0 and stage.shape[2] == dst.shape[1]

    def chunk_copy(i):
        return pltpu.make_async_copy(src.at[pl.ds(i * chunk_rows, chunk_rows), :],
                                     stage.at[i % 2], sem.at[i % 2])

    chunk_copy(0).start()
    for i in range(n_chunks):
        if i + 1 < n_chunks:
            chunk_copy(i + 1).start()
        chunk_copy(i).wait()
        dst[i * chunk_rows:(i + 1) * chunk_rows, :] = stage[i % 2].astype(BF16)


def _first_grid_step(n_axes):
    first = pl.program_id(0) == 0
    for axis in range(1, n_axes):
        first = jnp.logical_and(first, pl.program_id(axis) == 0)
    return first


def _normed_with_halo(h_scr, x_ref, xp_ref, xn_ref, g_ref):
    i = pl.program_id(1)
    gain = g_ref[...]

    def normed(v):
        return v * _rms_scale(v) * gain

    keep_prev = (i > 0).astype(F32)
    keep_next = (i < pl.num_programs(1) - 1).astype(F32)
    h_scr[0:HALO, :] = (normed(xp_ref[0]) * keep_prev).astype(BF16)
    h_scr[HALO:HALO + ROW_TILE, :] = normed(x_ref[0]).astype(BF16)
    h_scr[HALO + ROW_TILE:, :] = (normed(xn_ref[0]) * keep_next).astype(BF16)


def _conv3(ext, w):
    n = ROW_TILE // SUBLANES
    c = ext.shape[1]
    win = ext[HALO - SUBLANES:HALO + ROW_TILE + SUBLANES].reshape(n + 2, SUBLANES, c)
    sub = lax.broadcasted_iota(jnp.int32, (n, SUBLANES, c), 1)
    down = pltpu.roll(win, 1, 1)
    up = pltpu.roll(win, SUBLANES - 1, 1)
    prev = jnp.where(sub == 0, down[0:n], down[1:n + 1])
    nxt = jnp.where(sub == SUBLANES - 1, up[2:n + 2], up[1:n + 1])
    out = prev * w[0:1, :][None] + win[1:n + 1] * w[1:2, :][None] + nxt * w[2:3, :][None]
    return out.reshape(ROW_TILE, c)


def _mixer_kernel(x_ref, xp_ref, xn_ref, o_ref, pre_g_ref, w_in_hbm, gate_b_ref, conv_w_ref,
                  w_attn_hbm, w_conv_hbm, w_out_hbm, post_g_ref, out_ref, h_scr, w_scr, w_stage,
                  w_sem):
    d = x_ref.shape[2]
    n_in = MIXER_IN_BLOCKS
    col0 = (N_HEADS + 2 * N_KV_HEADS) * HEAD_DIM

    @pl.when(_first_grid_step(2))
    def _():
        for k in range(n_in):
            _load_weight_bf16(w_in_hbm.at[:, pl.ds(col0 + k * d, d)], w_scr.at[k], w_stage, w_sem)
        for k, w_hbm in enumerate((w_attn_hbm, w_conv_hbm, w_out_hbm)):
            _load_weight_bf16(w_hbm, w_scr.at[n_in + k], w_stage, w_sem)

    w_u, w_b, w_c, w_ga, w_gb, w_attn, w_conv, w_out = (w_scr.at[k] for k in range(n_in + 3))
    _normed_with_halo(h_scr, x_ref, xp_ref, xn_ref, pre_g_ref)
    h_ext = h_scr[...]
    u = jnp.dot(h_ext, w_u[...], preferred_element_type=F32)
    c = jnp.dot(h_ext, w_c[...], preferred_element_type=F32)
    h = h_scr[pl.ds(HALO, ROW_TILE), :]
    b_gate = jnp.dot(h, w_b[...], preferred_element_type=F32)
    conv = (b_gate * _conv3(c * u, conv_w_ref[...])).astype(BF16)
    y_b = jnp.dot(conv, w_conv[...], preferred_element_type=F32)
    attn = jnp.concatenate([o_ref[0, hh] for hh in range(N_HEADS)], axis=1)
    y_a = jnp.dot(attn, w_attn[...], preferred_element_type=F32)
    g_a = _sigmoid(jnp.dot(h, w_ga[...], preferred_element_type=F32) + gate_b_ref[:, :d])
    g_b = _sigmoid(jnp.dot(h, w_gb[...], preferred_element_type=F32) + gate_b_ref[:, d:])
    merged = (g_a * y_a + g_b * y_b).astype(BF16)
    out = jnp.dot(merged, w_out[...], preferred_element_type=F32)
    out_ref[0] = x_ref[0] + out * _rms_scale(out) * post_g_ref[...]


def _mixer(x, o, pre_g, w_in, gate_b, conv_w, w_attn, w_conv, w_out, post_g):
    b, s, d = x.shape
    main, prev, nxt = _halo_specs(s, d)
    const = lambda bi, i: (0, 0)
    full = lambda a: pl.BlockSpec(a.shape, const)
    hbm = pl.BlockSpec(memory_space=pl.ANY)
    return pl.pallas_call(
        _mixer_kernel,
        out_shape=jax.ShapeDtypeStruct((b, s, d), F32),
        grid=(b, s // ROW_TILE),
        in_specs=[main, prev, nxt,
                  pl.BlockSpec((1, N_HEADS, ROW_TILE, HEAD_DIM), lambda bi, i: (bi, 0, i, 0)),
                  full(pre_g), hbm, full(gate_b), full(conv_w), hbm, hbm, hbm, full(post_g)],
        out_specs=pl.BlockSpec((1, ROW_TILE, d), lambda bi, i: (bi, i, 0)),
        scratch_shapes=[
            pltpu.VMEM((ROW_TILE + 2 * HALO, d), BF16),
            pltpu.VMEM((MIXER_IN_BLOCKS + 3, d, d), BF16),
            _weight_stage(d, d),
            pltpu.SemaphoreType.DMA((2,)),
        ],
        compiler_params=pltpu.CompilerParams(
            dimension_semantics=("arbitrary", "arbitrary"),
            vmem_limit_bytes=V7X_VMEM_LIMIT_BYTES),
        name="mixer",
    )(x, x, x, o, pre_g, w_in, gate_b, conv_w, w_attn, w_conv, w_out, post_g)


def _ffn_kernel(x_ref, xp_ref, xn_ref, pre_g_ref, w_up_hbm, conv_w_ref, w_d_hbm, post_g_ref,
                out_ref, h_scr, hid_scr, w_up_ref, w_d_ref, up_stage, d_stage, w_sem):
    d_ff = conv_w_ref.shape[1]

    @pl.when(_first_grid_step(2))
    def _():
        _load_weight_bf16(w_up_hbm, w_up_ref, up_stage, w_sem)
        _load_weight_bf16(w_d_hbm, w_d_ref, d_stage, w_sem)

    _normed_with_halo(h_scr, x_ref, xp_ref, xn_ref, pre_g_ref)
    h_ext = h_scr[...]
    h = h_scr[pl.ds(HALO, ROW_TILE), :]
    for j in range(d_ff // FF_CHUNK):
        lo, hi = j * FF_CHUNK, (j + 1) * FF_CHUNK
        a_ext = jnp.dot(h_ext, w_up_ref[:, lo:hi], preferred_element_type=F32)
        val = jnp.dot(h, w_up_ref[:, d_ff + lo:d_ff + hi], preferred_element_type=F32)
        gate = _gelu_tanh(_conv3(a_ext, conv_w_ref[:, lo:hi]))
        hid_scr[:, lo:hi] = (gate * val).astype(BF16)
    out = jnp.dot(hid_scr[...], w_d_ref[...], preferred_element_type=F32)
    out_ref[0] = x_ref[0] + out * _rms_scale(out) * post_g_ref[...]


def _ffn(x, pre_g, w_up, conv_w, w_d, post_g):
    b, s, d = x.shape
    main, prev, nxt = _halo_specs(s, d)
    const2 = lambda bi, i: (0, 0)
    hbm = pl.BlockSpec(memory_space=pl.ANY)
    return pl.pallas_call(
        _ffn_kernel,
        out_shape=jax.ShapeDtypeStruct((b, s, d), F32),
        grid=(b, s // ROW_TILE),
        in_specs=[main, prev, nxt,
                  pl.BlockSpec(pre_g.shape, const2),
                  hbm,
                  pl.BlockSpec(conv_w.shape, const2),
                  hbm,
                  pl.BlockSpec(post_g.shape, const2)],
        out_specs=pl.BlockSpec((1, ROW_TILE, d), lambda bi, i: (bi, i, 0)),
        scratch_shapes=[
            pltpu.VMEM((ROW_TILE + 2 * HALO, d), BF16),
            pltpu.VMEM((ROW_TILE, w_d.shape[0]), BF16),
            pltpu.VMEM(w_up.shape, BF16),
            pltpu.VMEM(w_d.shape, BF16),
            _weight_stage(*w_up.shape),
            _weight_stage(*w_d.shape),
            pltpu.SemaphoreType.DMA((2,)),
        ],
        compiler_params=pltpu.CompilerParams(
            dimension_semantics=("arbitrary", "arbitrary"),
            vmem_limit_bytes=V7X_VMEM_LIMIT_BYTES),
        name="ffn",
    )(x, x, x, pre_g, w_up, conv_w, w_d, post_g)


def _rope_tables(seq):
    t = np.arange(seq)
    freqs = ROPE_THETA ** (-np.arange(ROT_HALF, dtype=np.float64) / ROT_HALF)

    def axis_tables(pos):
        ang = pos.astype(np.float64)[:, None] * freqs[None, :]
        ang = np.concatenate([ang, ang], axis=-1)
        return np.cos(ang), np.sin(ang)

    cos_r, sin_r = axis_tables(t // GRID_W)
    cos_c, sin_c = axis_tables(t % GRID_W)
    sign = np.where(np.arange(AXIS_ROPE_DIM) < ROT_HALF, -1.0, 1.0)
    cos = np.concatenate([cos_r, cos_c], axis=-1).astype(np.float32)
    sin = np.concatenate([sin_r * sign, sin_c * sign], axis=-1).astype(np.float32)
    return cos, sin


def _tile_transposed(table):
    seq = table.shape[0]
    return np.ascontiguousarray(table.reshape(seq // Q_ROWS, Q_ROWS, HEAD_DIM).transpose(0, 2, 1))


def kernel(x, mix_pre_g, w_in, gate_b, q_norm_g, k_norm_g, mix_conv_w, w_attn_proj, w_conv_proj,
           w_out, mix_post_g, ffn_pre_g, w_up, ffn_conv_w, w_down, ffn_post_g):
    depth = w_in.shape[0]
    seq, d = x.shape[1], x.shape[2]
    q_w = N_HEADS * HEAD_DIM
    kv_w = N_KV_HEADS * HEAD_DIM
    d_conv = mix_conv_w.shape[2]
    cos, sin = _rope_tables(seq)
    cos_t, sin_t = _tile_transposed(cos), _tile_transposed(sin)
    assert d_conv == d and w_in.shape[2] == q_w + 2 * kv_w + MIXER_IN_BLOCKS * d
    for l in range(depth):
        qg_t = jnp.broadcast_to(q_norm_g[l][:, None], (HEAD_DIM, Q_ROWS))
        score_bound = (HEAD_DIM * SCALE * LOG2_E * 1.01
                       * jnp.max(jnp.abs(q_norm_g[l])) * jnp.max(jnp.abs(k_norm_g[l])))
        shift = jnp.logical_not(score_bound <= EXP2_SAFE_SCORE).astype(jnp.int32).reshape(1)
        attn = _attention(shift, x, mix_pre_g[l][None], w_in[l], qg_t, k_norm_g[l][None],
                          cos, sin, cos_t, sin_t)
        x = _mixer(x, attn, mix_pre_g[l][None], w_in[l], gate_b[l][None], mix_conv_w[l],
                   w_attn_proj[l], w_conv_proj[l], w_out[l], mix_post_g[l][None])
        x = _ffn(x, ffn_pre_g[l][None], w_up[l], ffn_conv_w[l], w_down[l], ffn_post_g[l][None])
    return x
```

```python
import numpy as np

import jax
import jax.numpy as jnp
from jax import lax
from jax.experimental import pallas as pl
from jax.experimental.pallas import tpu as pltpu

N_HEADS = 8
N_KV_HEADS = 2
HEAD_DIM = 128
GROUP = N_HEADS // N_KV_HEADS
GRID_W = 64
ROPE_THETA = 10000.0
AXIS_ROPE_DIM = HEAD_DIM // 2
ROT_HALF = AXIS_ROPE_DIM // 2
EPS = 1e-6
SCALE = HEAD_DIM ** -0.5
LOG2_E = 1.4426950408889634

SUBLANES = 8
HALO = 16
ONES_ROWS = 16
V7X_VMEM_LIMIT_BYTES = 56 * 1024 * 1024

ROW_TILE = 1024
Q_ROWS = 512
ITEMS_PER_STEP = 8
EXP2_SAFE_SCORE = 64.0
FF_CHUNK = 256

F32 = jnp.float32
BF16 = jnp.bfloat16


def _rms_scale(v):
    return lax.rsqrt(jnp.mean(v * v, axis=-1, keepdims=True) + EPS)


def _sigmoid(v):
    return 1.0 / (1.0 + jnp.exp(-v))


def _gelu_tanh(v):
    c = 0.7978845608028654
    return 0.5 * v * (1.0 + jnp.tanh(c * (v + 0.044715 * (v * v * v))))


def _attention_kernel(shift_ref, x_hbm, g_ref, wt_ref, qg_t_ref, kg_t_ref, cos_t_ref, sin_t_ref,
                      o_ref, qt_scr, k_scr, vt_scr, s_even, s_odd, m_even, m_odd, x_buf, x_sem):
    seq = x_hbm.shape[1]
    n_tiles = seq // Q_ROWS
    batch = pl.program_id(0)
    n_items = N_HEADS * n_tiles
    tile_bits = n_tiles.bit_length() - 1
    group_bits = GROUP.bit_length() - 1
    q_w = N_HEADS * HEAD_DIM
    kv_w = N_KV_HEADS * HEAD_DIM

    def rot_half(v):
        r = ROT_HALF
        return jnp.concatenate([v[r:2 * r], v[0:r], v[3 * r:4 * r], v[2 * r:3 * r]], axis=0)

    def rope_cols(zt, cos_g, sin_g, out_scale):
        inv = lax.rsqrt(jnp.mean(zt * zt, axis=0, keepdims=True) + EPS) * out_scale
        zn = zt * inv
        return zn * cos_g + rot_half(zn) * sin_g

    def x_copy(bi):
        return pltpu.make_async_copy(x_hbm.at[bi], x_buf, x_sem.at[0])

    def proj_step(r):
        rows = pl.ds(r * Q_ROWS, Q_ROWS)
        xt = x_buf[rows, :]
        h_t = (xt * _rms_scale(xt) * g_ref[...]).T.astype(BF16)
        z_t = jnp.dot(wt_ref[...], h_t, preferred_element_type=F32)
        cos_t = cos_t_ref[r]
        sin_t = sin_t_ref[r]
        cos_q, sin_q = qg_t_ref[...] * cos_t, rot_half(qg_t_ref[...]) * sin_t
        cos_k, sin_k = kg_t_ref[...] * cos_t, rot_half(kg_t_ref[...]) * sin_t
        for hh in range(N_HEADS):
            zq_t = z_t[hh * HEAD_DIM:(hh + 1) * HEAD_DIM]
            qt_scr[hh, r] = rope_cols(zq_t, cos_q, sin_q, SCALE * LOG2_E).astype(BF16)
        for kh in range(N_KV_HEADS):
            zk_t = z_t[q_w + kh * HEAD_DIM:q_w + (kh + 1) * HEAD_DIM]
            k_scr[kh, rows, :] = rope_cols(zk_t, cos_k, sin_k, 1.0).T.astype(BF16)
            zv_t = z_t[q_w + kv_w + kh * HEAD_DIM:q_w + kv_w + (kh + 1) * HEAD_DIM]
            vt_scr[kh, r, 0:HEAD_DIM, :] = zv_t.astype(BF16)
            vt_scr[kh, r, HEAD_DIM:, :] = jnp.ones((ONES_ROWS, Q_ROWS), BF16)

    @pl.when(batch == 0)
    def _():
        x_copy(0).start()

    x_copy(batch).wait()
    for r in range(n_tiles):
        proj_step(r)

    def split(item):
        hh = item >> tile_bits
        t = item & (n_tiles - 1)
        kh = hh >> group_bits
        return hh, t, kh

    def tile_rows(t):
        start = t * Q_ROWS
        if not isinstance(start, int):
            start = pl.multiple_of(start, Q_ROWS)
        return pl.ds(start, Q_ROWS)

    def scores(item, s_scr, m_scr):
        hh, t, kh = split(item)
        s_t = jnp.dot(k_scr[kh], qt_scr[hh, t], preferred_element_type=F32)
        s_scr[...] = s_t
        m_scr[...] = jnp.max(s_t, axis=0, keepdims=True)

    def finish(item, s_scr, m_scr):
        attend(item, jnp.exp2(s_scr[...] - m_scr[...]).astype(BF16))

    def attend(item, p16, denom=None):
        hh, t, kh = split(item)
        v_rows = HEAD_DIM + (ONES_ROWS if denom is None else 0)
        acc = jnp.dot(vt_scr[kh, 0, 0:v_rows, :], p16[0:Q_ROWS], preferred_element_type=F32)
        for kt in range(1, n_tiles):
            acc = acc + jnp.dot(vt_scr[kh, kt, 0:v_rows, :], p16[kt * Q_ROWS:(kt + 1) * Q_ROWS],
                                preferred_element_type=F32)
        if denom is None:
            denom = acc[HEAD_DIM:HEAD_DIM + 1]
        o = (acc[0:HEAD_DIM] * (1.0 / denom)).T
        o_ref[0, hh, tile_rows(t), :] = o.astype(BF16)

    def pair_step(j, carry):
        scores(2 * j + 1, s_odd, m_odd)
        finish(2 * j, s_even, m_even)
        scores(2 * j + 2, s_even, m_even)
        finish(2 * j + 1, s_odd, m_odd)
        return carry

    def unshifted_step(j, carry):
        for i in range(ITEMS_PER_STEP):
            item = ITEMS_PER_STEP * j + i
            hh, t, kh = split(item)
            p_t = jnp.exp2(jnp.dot(k_scr[kh], qt_scr[hh, t], preferred_element_type=F32))
            attend(item, p_t.astype(BF16), jnp.sum(p_t, axis=0, keepdims=True))
        return carry

    @pl.when(batch + 1 < pl.num_programs(0))
    def _():
        x_copy(batch + 1).start()

    shift = shift_ref[0] != 0

    @pl.when(shift)
    def _():
        scores(0, s_even, m_even)
        lax.fori_loop(0, n_items // 2 - 1, pair_step, 0)
        scores(n_items - 1, s_odd, m_odd)
        finish(n_items - 2, s_even, m_even)
        finish(n_items - 1, s_odd, m_odd)

    @pl.when(jnp.logical_not(shift))
    def _():
        lax.fori_loop(0, n_items // ITEMS_PER_STEP, unshifted_step, 0)


def _attention(shift, x, pre_g, w_qkv_t, qg_t, kg_t, cos_t, sin_t):
    b, s, d = x.shape
    n_tiles = s // Q_ROWS
    const2 = lambda i: (0, 0)
    const3 = lambda i: (0, 0, 0)
    once = pl.Buffered(1)
    return pl.pallas_call(
        _attention_kernel,
        out_shape=jax.ShapeDtypeStruct((b, N_HEADS, s, HEAD_DIM), BF16),
        grid=(b,),
        in_specs=[
            pl.BlockSpec(memory_space=pltpu.SMEM),
            pl.BlockSpec(memory_space=pl.ANY),
            pl.BlockSpec((1, d), const2),
            pl.BlockSpec(w_qkv_t.shape, const2, pipeline_mode=once),
            pl.BlockSpec(qg_t.shape, const2),
            pl.BlockSpec(kg_t.shape, const2),
            pl.BlockSpec(cos_t.shape, const3, pipeline_mode=once),
            pl.BlockSpec(sin_t.shape, const3, pipeline_mode=once),
        ],
        out_specs=pl.BlockSpec((1, N_HEADS, s, HEAD_DIM), lambda i: (i, 0, 0, 0)),
        scratch_shapes=[
            pltpu.VMEM((N_HEADS, n_tiles, HEAD_DIM, Q_ROWS), BF16),
            pltpu.VMEM((N_KV_HEADS, s, HEAD_DIM), BF16),
            pltpu.VMEM((N_KV_HEADS, n_tiles, HEAD_DIM + ONES_ROWS, Q_ROWS), BF16),
            pltpu.VMEM((s, Q_ROWS), F32),
            pltpu.VMEM((s, Q_ROWS), F32),
            pltpu.VMEM((1, Q_ROWS), F32),
            pltpu.VMEM((1, Q_ROWS), F32),
            pltpu.VMEM((s, d), F32),
            pltpu.SemaphoreType.DMA((1,)),
        ],
        compiler_params=pltpu.CompilerParams(
            dimension_semantics=("arbitrary",), vmem_limit_bytes=V7X_VMEM_LIMIT_BYTES),
        name="attention",
    )(shift, x, pre_g, w_qkv_t, qg_t, kg_t, cos_t, sin_t)


def _halo_specs(seq, d):
    per_tile = ROW_TILE // HALO
    last = seq // HALO - 1
    main = pl.BlockSpec((1, ROW_TILE, d), lambda b, i: (b, i, 0))
    prev = pl.BlockSpec((1, HALO, d), lambda b, i: (b, jnp.maximum(i * per_tile - 1, 0), 0))
    nxt = pl.BlockSpec((1, HALO, d), lambda b, i: (b, jnp.minimum((i + 1) * per_tile, last), 0))
    return main, prev, nxt


def _normed_with_halo(h_scr, x_ref, xp_ref, xn_ref, g_ref):
    i = pl.program_id(1)
    gain = g_ref[...]

    def normed(v):
        return v * _rms_scale(v) * gain

    keep_prev = (i > 0).astype(F32)
    keep_next = (i < pl.num_programs(1) - 1).astype(F32)
    h_scr[0:HALO, :] = (normed(xp_ref[0]) * keep_prev).astype(BF16)
    h_scr[HALO:HALO + ROW_TILE, :] = normed(x_ref[0]).astype(BF16)
    h_scr[HALO + ROW_TILE:, :] = (normed(xn_ref[0]) * keep_next).astype(BF16)


def _conv3(ext, w):
    n = ROW_TILE // SUBLANES
    c = ext.shape[1]
    win = ext[HALO - SUBLANES:HALO + ROW_TILE + SUBLANES].reshape(n + 2, SUBLANES, c)
    sub = lax.broadcasted_iota(jnp.int32, (n, SUBLANES, c), 1)
    down = pltpu.roll(win, 1, 1)
    up = pltpu.roll(win, SUBLANES - 1, 1)
    prev = jnp.where(sub == 0, down[0:n], down[1:n + 1])
    nxt = jnp.where(sub == SUBLANES - 1, up[2:n + 2], up[1:n + 1])
    out = prev * w[0:1, :][None] + win[1:n + 1] * w[1:2, :][None] + nxt * w[2:3, :][None]
    return out.reshape(ROW_TILE, c)


def _mixer_kernel(x_ref, xp_ref, xn_ref, o_ref, pre_g_ref, w_u_ref, w_c_ref, w_b_ref,
                  w_ga_ref, w_gb_ref, gate_b_ref, conv_w_ref, w_attn_ref, w_conv_ref,
                  w_out_ref, post_g_ref, out_ref, h_scr):
    d = x_ref.shape[2]
    _normed_with_halo(h_scr, x_ref, xp_ref, xn_ref, pre_g_ref)
    h_ext = h_scr[...]
    u = jnp.dot(h_ext, w_u_ref[...], preferred_element_type=F32)
    c = jnp.dot(h_ext, w_c_ref[...], preferred_element_type=F32)
    h = h_scr[pl.ds(HALO, ROW_TILE), :]
    b_gate = jnp.dot(h, w_b_ref[...], preferred_element_type=F32)
    conv = (b_gate * _conv3(c * u, conv_w_ref[...])).astype(BF16)
    y_b = jnp.dot(conv, w_conv_ref[...], preferred_element_type=F32)
    attn = jnp.concatenate([o_ref[0, hh] for hh in range(N_HEADS)], axis=1)
    y_a = jnp.dot(attn, w_attn_ref[...], preferred_element_type=F32)
    g_a = _sigmoid(jnp.dot(h, w_ga_ref[...], preferred_element_type=F32) + gate_b_ref[:, :d])
    g_b = _sigmoid(jnp.dot(h, w_gb_ref[...], preferred_element_type=F32) + gate_b_ref[:, d:])
    merged = (g_a * y_a + g_b * y_b).astype(BF16)
    out = jnp.dot(merged, w_out_ref[...], preferred_element_type=F32)
    out_ref[0] = x_ref[0] + out * _rms_scale(out) * post_g_ref[...]


def _mixer(x, o, pre_g, w_u, w_c, w_b, w_ga, w_gb, gate_b, conv_w, w_attn, w_conv, w_out, post_g):
    b, s, d = x.shape
    main, prev, nxt = _halo_specs(s, d)
    const = lambda bi, i: (0, 0)
    full = lambda a: pl.BlockSpec(a.shape, const)
    return pl.pallas_call(
        _mixer_kernel,
        out_shape=jax.ShapeDtypeStruct((b, s, d), F32),
        grid=(b, s // ROW_TILE),
        in_specs=[main, prev, nxt,
                  pl.BlockSpec((1, N_HEADS, ROW_TILE, HEAD_DIM), lambda bi, i: (bi, 0, i, 0)),
                  full(pre_g), full(w_u), full(w_c), full(w_b), full(w_ga), full(w_gb),
                  full(gate_b), full(conv_w), full(w_attn), full(w_conv), full(w_out),
                  full(post_g)],
        out_specs=pl.BlockSpec((1, ROW_TILE, d), lambda bi, i: (bi, i, 0)),
        scratch_shapes=[pltpu.VMEM((ROW_TILE + 2 * HALO, d), BF16)],
        compiler_params=pltpu.CompilerParams(
            dimension_semantics=("arbitrary", "arbitrary"),
            vmem_limit_bytes=V7X_VMEM_LIMIT_BYTES),
        name="mixer",
    )(x, x, x, o, pre_g, w_u, w_c, w_b, w_ga, w_gb, gate_b, conv_w, w_attn, w_conv, w_out, post_g)


def _ffn_kernel(x_ref, xp_ref, xn_ref, pre_g_ref, w_up_ref, conv_w_ref, w_d_ref, post_g_ref,
                out_ref, h_scr, hid_scr):
    d_ff = conv_w_ref.shape[1]
    _normed_with_halo(h_scr, x_ref, xp_ref, xn_ref, pre_g_ref)
    h_ext = h_scr[...]
    h = h_scr[pl.ds(HALO, ROW_TILE), :]
    for j in range(d_ff // FF_CHUNK):
        lo, hi = j * FF_CHUNK, (j + 1) * FF_CHUNK
        a_ext = jnp.dot(h_ext, w_up_ref[:, lo:hi], preferred_element_type=F32)
        val = jnp.dot(h, w_up_ref[:, d_ff + lo:d_ff + hi], preferred_element_type=F32)
        gate = _gelu_tanh(_conv3(a_ext, conv_w_ref[:, lo:hi]))
        hid_scr[:, lo:hi] = (gate * val).astype(BF16)
    out = jnp.dot(hid_scr[...], w_d_ref[...], preferred_element_type=F32)
    out_ref[0] = x_ref[0] + out * _rms_scale(out) * post_g_ref[...]


def _ffn(x, pre_g, w_up, conv_w, w_d, post_g):
    b, s, d = x.shape
    main, prev, nxt = _halo_specs(s, d)
    const2 = lambda bi, i: (0, 0)
    once = pl.Buffered(1)
    return pl.pallas_call(
        _ffn_kernel,
        out_shape=jax.ShapeDtypeStruct((b, s, d), F32),
        grid=(b, s // ROW_TILE),
        in_specs=[main, prev, nxt,
                  pl.BlockSpec(pre_g.shape, const2),
                  pl.BlockSpec(w_up.shape, const2, pipeline_mode=once),
                  pl.BlockSpec(conv_w.shape, const2),
                  pl.BlockSpec(w_d.shape, const2, pipeline_mode=once),
                  pl.BlockSpec(post_g.shape, const2)],
        out_specs=pl.BlockSpec((1, ROW_TILE, d), lambda bi, i: (bi, i, 0)),
        scratch_shapes=[
            pltpu.VMEM((ROW_TILE + 2 * HALO, d), BF16),
            pltpu.VMEM((ROW_TILE, w_d.shape[0]), BF16),
        ],
        compiler_params=pltpu.CompilerParams(
            dimension_semantics=("arbitrary", "arbitrary"),
            vmem_limit_bytes=V7X_VMEM_LIMIT_BYTES),
        name="ffn",
    )(x, x, x, pre_g, w_up, conv_w, w_d, post_g)


def _rope_tables_t(seq):
    t = np.arange(seq)
    freqs = ROPE_THETA ** (-np.arange(ROT_HALF, dtype=np.float64) / ROT_HALF)

    def axis_tables(pos):
        ang = pos.astype(np.float64)[:, None] * freqs[None, :]
        ang = np.concatenate([ang, ang], axis=-1)
        return np.cos(ang), np.sin(ang)

    cos_r, sin_r = axis_tables(t // GRID_W)
    cos_c, sin_c = axis_tables(t % GRID_W)
    sign = np.where(np.arange(AXIS_ROPE_DIM) < ROT_HALF, -1.0, 1.0)
    cos = np.concatenate([cos_r, cos_c], axis=-1).astype(np.float32)
    sin = np.concatenate([sin_r * sign, sin_c * sign], axis=-1).astype(np.float32)

    def tiled(table):
        return np.ascontiguousarray(
            table.reshape(seq // Q_ROWS, Q_ROWS, HEAD_DIM).transpose(0, 2, 1))

    return tiled(cos), tiled(sin)


def kernel(x, mix_pre_g, w_in, gate_b, q_norm_g, k_norm_g, mix_conv_w, w_attn_proj, w_conv_proj,
           w_out, mix_post_g, ffn_pre_g, w_up, ffn_conv_w, w_down, ffn_post_g):
    depth = w_in.shape[0]
    seq, d = x.shape[1], x.shape[2]
    q_w = N_HEADS * HEAD_DIM
    kv_w = N_KV_HEADS * HEAD_DIM
    d_conv = mix_conv_w.shape[2]
    cos_t, sin_t = _rope_tables_t(seq)
    for l in range(depth):
        w = w_in[l].astype(BF16)
        o0 = q_w + 2 * kv_w
        w_qkv_t = w[:, :o0].T
        w_u = w[:, o0:o0 + d_conv]
        w_b = w[:, o0 + d_conv:o0 + 2 * d_conv]
        w_c = w[:, o0 + 2 * d_conv:o0 + 3 * d_conv]
        w_ga = w[:, o0 + 3 * d_conv:o0 + 3 * d_conv + d]
        w_gb = w[:, o0 + 3 * d_conv + d:]
        qg_t = jnp.broadcast_to(q_norm_g[l][:, None], (HEAD_DIM, Q_ROWS))
        kg_t = jnp.broadcast_to(k_norm_g[l][:, None], (HEAD_DIM, Q_ROWS))
        score_bound = (HEAD_DIM * SCALE * LOG2_E * 1.01
                       * jnp.max(jnp.abs(q_norm_g[l])) * jnp.max(jnp.abs(k_norm_g[l])))
        shift = jnp.logical_not(score_bound <= EXP2_SAFE_SCORE).astype(jnp.int32).reshape(1)
        attn = _attention(shift, x, mix_pre_g[l][None], w_qkv_t, qg_t, kg_t, cos_t, sin_t)
        x = _mixer(x, attn, mix_pre_g[l][None], w_u, w_c, w_b, w_ga, w_gb, gate_b[l][None],
                   mix_conv_w[l], w_attn_proj[l].astype(BF16), w_conv_proj[l].astype(BF16),
                   w_out[l].astype(BF16), mix_post_g[l][None])
        x = _ffn(x, ffn_pre_g[l][None], w_up[l].astype(BF16), ffn_conv_w[l],
                 w_down[l].astype(BF16), ffn_post_g[l][None])
    return x
```

```python
import numpy as np

import jax
import jax.numpy as jnp
from jax import lax
from jax.experimental import pallas as pl
from jax.experimental.pallas import tpu as pltpu

N_HEADS = 8
N_KV_HEADS = 2
HEAD_DIM = 128
GROUP = N_HEADS // N_KV_HEADS
GRID_W = 64
ROPE_THETA = 10000.0
AXIS_ROPE_DIM = HEAD_DIM // 2
ROT_HALF = AXIS_ROPE_DIM // 2
EPS = 1e-6
SCALE = HEAD_DIM ** -0.5
LOG2_E = 1.4426950408889634

SUBLANES = 8
HALO = 16
ONES_ROWS = 16
V7X_VMEM_LIMIT_BYTES = 56 * 1024 * 1024

ROW_TILE = 1024
Q_ROWS = 512
ITEMS_PER_STEP = 8
EXP2_SAFE_SCORE = 64.0
FF_CHUNK = 256

F32 = jnp.float32
BF16 = jnp.bfloat16


def _rms_scale(v):
    return lax.rsqrt(jnp.mean(v * v, axis=-1, keepdims=True) + EPS)


def _sigmoid(v):
    return 1.0 / (1.0 + jnp.exp(-v))


def _gelu_tanh(v):
    c = 0.7978845608028654
    return 0.5 * v * (1.0 + jnp.tanh(c * (v + 0.044715 * (v * v * v))))


def _attention_kernel(shift_ref, x_hbm, g_ref, wt_ref, qg_t_ref, kg_t_ref, cos_t_ref, sin_t_ref,
                      o_ref, qt_scr, k_scr, vt_scr, s_even, s_odd, m_even, m_odd, x_buf, x_sem):
    seq = x_hbm.shape[1]
    n_tiles = seq // Q_ROWS
    batch = pl.program_id(0)
    n_items = N_HEADS * n_tiles
    tile_bits = n_tiles.bit_length() - 1
    group_bits = GROUP.bit_length() - 1
    q_w = N_HEADS * HEAD_DIM
    kv_w = N_KV_HEADS * HEAD_DIM

    def rot_half(v):
        r = ROT_HALF
        return jnp.concatenate([v[r:2 * r], v[0:r], v[3 * r:4 * r], v[2 * r:3 * r]], axis=0)

    def rope_cols(zt, cos_g, sin_g, out_scale):
        inv = lax.rsqrt(jnp.mean(zt * zt, axis=0, keepdims=True) + EPS) * out_scale
        zn = zt * inv
        return zn * cos_g + rot_half(zn) * sin_g

    def x_copy(bi):
        return pltpu.make_async_copy(x_hbm.at[bi], x_buf, x_sem.at[0])

    def proj_step(r):
        rows = pl.ds(r * Q_ROWS, Q_ROWS)
        xt = x_buf[rows, :]
        h = (xt * _rms_scale(xt) * g_ref[...]).astype(BF16)
        z_t = lax.dot_general(wt_ref[...], h, (((1,), (1,)), ((), ())),
                              preferred_element_type=F32)
        cos_t = cos_t_ref[r]
        sin_t = sin_t_ref[r]
        cos_q, sin_q = qg_t_ref[...] * cos_t, rot_half(qg_t_ref[...]) * sin_t
        cos_k, sin_k = kg_t_ref[...] * cos_t, rot_half(kg_t_ref[...]) * sin_t
        for hh in range(N_HEADS):
            zq_t = z_t[hh * HEAD_DIM:(hh + 1) * HEAD_DIM]
            qt_scr[hh, r] = rope_cols(zq_t, cos_q, sin_q, SCALE * LOG2_E).astype(BF16)
        for kh in range(N_KV_HEADS):
            zk_t = z_t[q_w + kh * HEAD_DIM:q_w + (kh + 1) * HEAD_DIM]
            k_scr[kh, rows, :] = rope_cols(zk_t, cos_k, sin_k, 1.0).T.astype(BF16)
            zv_t = z_t[q_w + kv_w + kh * HEAD_DIM:q_w + kv_w + (kh + 1) * HEAD_DIM]
            vt_scr[kh, r, 0:HEAD_DIM, :] = zv_t.astype(BF16)
            vt_scr[kh, r, HEAD_DIM:, :] = jnp.ones((ONES_ROWS, Q_ROWS), BF16)

    @pl.when(batch == 0)
    def _():
        x_copy(0).start()

    x_copy(batch).wait()
    for r in range(n_tiles):
        proj_step(r)

    def split(item):
        hh = item >> tile_bits
        t = item & (n_tiles - 1)
        kh = hh >> group_bits
        return hh, t, kh

    def tile_rows(t):
        start = t * Q_ROWS
        if not isinstance(start, int):
            start = pl.multiple_of(start, Q_ROWS)
        return pl.ds(start, Q_ROWS)

    def scores(item, s_scr, m_scr):
        hh, t, kh = split(item)
        s_t = jnp.dot(k_scr[kh], qt_scr[hh, t], preferred_element_type=F32)
        s_scr[...] = s_t
        m_scr[...] = jnp.max(s_t, axis=0, keepdims=True)

    def finish(item, s_scr, m_scr):
        attend(item, jnp.exp2(s_scr[...] - m_scr[...]).astype(BF16))

    def attend(item, p16, denom=None):
        hh, t, kh = split(item)
        v_rows = HEAD_DIM + (ONES_ROWS if denom is None else 0)
        acc = jnp.dot(vt_scr[kh, 0, 0:v_rows, :], p16[0:Q_ROWS], preferred_element_type=F32)
        for kt in range(1, n_tiles):
            acc = acc + jnp.dot(vt_scr[kh, kt, 0:v_rows, :], p16[kt * Q_ROWS:(kt + 1) * Q_ROWS],
                                preferred_element_type=F32)
        if denom is None:
            denom = acc[HEAD_DIM:HEAD_DIM + 1]
        o_ref[0, hh, t] = (acc[0:HEAD_DIM] * (1.0 / denom)).astype(BF16)

    def pair_step(j, carry):
        scores(2 * j + 1, s_odd, m_odd)
        finish(2 * j, s_even, m_even)
        scores(2 * j + 2, s_even, m_even)
        finish(2 * j + 1, s_odd, m_odd)
        return carry

    def unshifted_step(j, carry):
        for i in range(ITEMS_PER_STEP):
            item = ITEMS_PER_STEP * j + i
            hh, t, kh = split(item)
            p_t = jnp.exp2(jnp.dot(k_scr[kh], qt_scr[hh, t], preferred_element_type=F32))
            attend(item, p_t.astype(BF16), jnp.sum(p_t, axis=0, keepdims=True))
        return carry

    @pl.when(batch + 1 < pl.num_programs(0))
    def _():
        x_copy(batch + 1).start()

    shift = shift_ref[0] != 0

    @pl.when(shift)
    def _():
        scores(0, s_even, m_even)
        lax.fori_loop(0, n_items // 2 - 1, pair_step, 0)
        scores(n_items - 1, s_odd, m_odd)
        finish(n_items - 2, s_even, m_even)
        finish(n_items - 1, s_odd, m_odd)

    @pl.when(jnp.logical_not(shift))
    def _():
        lax.fori_loop(0, n_items // ITEMS_PER_STEP, unshifted_step, 0)


def _attention(shift, x, pre_g, w_qkv_t, qg_t, kg_t, cos_t, sin_t):
    b, s, d = x.shape
    n_tiles = s // Q_ROWS
    const2 = lambda i: (0, 0)
    const3 = lambda i: (0, 0, 0)
    once = pl.Buffered(1)
    return pl.pallas_call(
        _attention_kernel,
        out_shape=jax.ShapeDtypeStruct((b, N_HEADS, n_tiles, HEAD_DIM, Q_ROWS), BF16),
        grid=(b,),
        in_specs=[
            pl.BlockSpec(memory_space=pltpu.SMEM),
            pl.BlockSpec(memory_space=pl.ANY),
            pl.BlockSpec((1, d), const2),
            pl.BlockSpec(w_qkv_t.shape, const2, pipeline_mode=once),
            pl.BlockSpec(qg_t.shape, const2),
            pl.BlockSpec(kg_t.shape, const2),
            pl.BlockSpec(cos_t.shape, const3, pipeline_mode=once),
            pl.BlockSpec(sin_t.shape, const3, pipeline_mode=once),
        ],
        out_specs=pl.BlockSpec((1, N_HEADS, n_tiles, HEAD_DIM, Q_ROWS),
                               lambda i: (i, 0, 0, 0, 0)),
        scratch_shapes=[
            pltpu.VMEM((N_HEADS, n_tiles, HEAD_DIM, Q_ROWS), BF16),
            pltpu.VMEM((N_KV_HEADS, s, HEAD_DIM), BF16),
            pltpu.VMEM((N_KV_HEADS, n_tiles, HEAD_DIM + ONES_ROWS, Q_ROWS), BF16),
            pltpu.VMEM((s, Q_ROWS), F32),
            pltpu.VMEM((s, Q_ROWS), F32),
            pltpu.VMEM((1, Q_ROWS), F32),
            pltpu.VMEM((1, Q_ROWS), F32),
            pltpu.VMEM((s, d), F32),
            pltpu.SemaphoreType.DMA((1,)),
        ],
        compiler_params=pltpu.CompilerParams(
            dimension_semantics=("arbitrary",), vmem_limit_bytes=V7X_VMEM_LIMIT_BYTES),
        name="attention",
    )(shift, x, pre_g, w_qkv_t, qg_t, kg_t, cos_t, sin_t)


def _halo_specs(seq, d):
    per_tile = ROW_TILE // HALO
    last = seq // HALO - 1
    main = pl.BlockSpec((1, ROW_TILE, d), lambda b, i: (b, i, 0))
    prev = pl.BlockSpec((1, HALO, d), lambda b, i: (b, jnp.maximum(i * per_tile - 1, 0), 0))
    nxt = pl.BlockSpec((1, HALO, d), lambda b, i: (b, jnp.minimum((i + 1) * per_tile, last), 0))
    return main, prev, nxt


def _normed_with_halo(h_scr, x_ref, xp_ref, xn_ref, g_ref):
    i = pl.program_id(1)
    gain = g_ref[...]

    def normed(v):
        return v * _rms_scale(v) * gain

    keep_prev = (i > 0).astype(F32)
    keep_next = (i < pl.num_programs(1) - 1).astype(F32)
    h_scr[0:HALO, :] = (normed(xp_ref[0]) * keep_prev).astype(BF16)
    h_scr[HALO:HALO + ROW_TILE, :] = normed(x_ref[0]).astype(BF16)
    h_scr[HALO + ROW_TILE:, :] = (normed(xn_ref[0]) * keep_next).astype(BF16)


def _conv3(ext, w):
    n = ROW_TILE // SUBLANES
    c = ext.shape[1]
    win = ext[HALO - SUBLANES:HALO + ROW_TILE + SUBLANES].reshape(n + 2, SUBLANES, c)
    sub = lax.broadcasted_iota(jnp.int32, (n, SUBLANES, c), 1)
    down = pltpu.roll(win, 1, 1)
    up = pltpu.roll(win, SUBLANES - 1, 1)
    prev = jnp.where(sub == 0, down[0:n], down[1:n + 1])
    nxt = jnp.where(sub == SUBLANES - 1, up[2:n + 2], up[1:n + 1])
    out = prev * w[0:1, :][None] + win[1:n + 1] * w[1:2, :][None] + nxt * w[2:3, :][None]
    return out.reshape(ROW_TILE, c)


def _mixer_kernel(x_ref, xp_ref, xn_ref, o_ref, pre_g_ref, w_u_ref, w_c_ref, w_b_ref,
                  w_ga_ref, w_gb_ref, gate_b_ref, conv_w_ref, w_attn_ref, w_conv_ref,
                  w_out_ref, post_g_ref, out_ref, h_scr):
    d = x_ref.shape[2]
    _normed_with_halo(h_scr, x_ref, xp_ref, xn_ref, pre_g_ref)
    h_ext = h_scr[...]
    u = jnp.dot(h_ext, w_u_ref[...], preferred_element_type=F32)
    c = jnp.dot(h_ext, w_c_ref[...], preferred_element_type=F32)
    h = h_scr[pl.ds(HALO, ROW_TILE), :]
    b_gate = jnp.dot(h, w_b_ref[...], preferred_element_type=F32)
    conv = (b_gate * _conv3(c * u, conv_w_ref[...])).astype(BF16)
    y_b = jnp.dot(conv, w_conv_ref[...], preferred_element_type=F32)
    attn_t = jnp.concatenate(
        [jnp.concatenate([o_ref[0, hh, tt] for tt in range(ROW_TILE // Q_ROWS)], axis=1)
         for hh in range(N_HEADS)], axis=0)
    y_a = lax.dot_general(attn_t, w_attn_ref[...], (((0,), (0,)), ((), ())),
                          preferred_element_type=F32)
    g_a = _sigmoid(jnp.dot(h, w_ga_ref[...], preferred_element_type=F32) + gate_b_ref[:, :d])
    g_b = _sigmoid(jnp.dot(h, w_gb_ref[...], preferred_element_type=F32) + gate_b_ref[:, d:])
    merged = (g_a * y_a + g_b * y_b).astype(BF16)
    out = jnp.dot(merged, w_out_ref[...], preferred_element_type=F32)
    out_ref[0] = x_ref[0] + out * _rms_scale(out) * post_g_ref[...]


def _mixer(x, o, pre_g, w_u, w_c, w_b, w_ga, w_gb, gate_b, conv_w, w_attn, w_conv, w_out, post_g):
    b, s, d = x.shape
    main, prev, nxt = _halo_specs(s, d)
    const = lambda bi, i: (0, 0)
    full = lambda a: pl.BlockSpec(a.shape, const)
    return pl.pallas_call(
        _mixer_kernel,
        out_shape=jax.ShapeDtypeStruct((b, s, d), F32),
        grid=(b, s // ROW_TILE),
        in_specs=[main, prev, nxt,
                  pl.BlockSpec((1, N_HEADS, ROW_TILE // Q_ROWS, HEAD_DIM, Q_ROWS),
                               lambda bi, i: (bi, 0, i, 0, 0)),
                  full(pre_g), full(w_u), full(w_c), full(w_b), full(w_ga), full(w_gb),
                  full(gate_b), full(conv_w), full(w_attn), full(w_conv), full(w_out),
                  full(post_g)],
        out_specs=pl.BlockSpec((1, ROW_TILE, d), lambda bi, i: (bi, i, 0)),
        scratch_shapes=[pltpu.VMEM((ROW_TILE + 2 * HALO, d), BF16)],
        compiler_params=pltpu.CompilerParams(
            dimension_semantics=("arbitrary", "arbitrary"),
            vmem_limit_bytes=V7X_VMEM_LIMIT_BYTES),
        name="mixer",
    )(x, x, x, o, pre_g, w_u, w_c, w_b, w_ga, w_gb, gate_b, conv_w, w_attn, w_conv, w_out, post_g)


def _ffn_kernel(x_ref, xp_ref, xn_ref, pre_g_ref, w_up_ref, conv_w_ref, w_d_ref, post_g_ref,
                out_ref, h_scr, hid_scr):
    d_ff = conv_w_ref.shape[1]
    _normed_with_halo(h_scr, x_ref, xp_ref, xn_ref, pre_g_ref)
    h_ext = h_scr[...]
    h = h_scr[pl.ds(HALO, ROW_TILE), :]
    for j in range(d_ff // FF_CHUNK):
        lo, hi = j * FF_CHUNK, (j + 1) * FF_CHUNK
        a_ext = jnp.dot(h_ext, w_up_ref[:, lo:hi], preferred_element_type=F32)
        val = jnp.dot(h, w_up_ref[:, d_ff + lo:d_ff + hi], preferred_element_type=F32)
        gate = _gelu_tanh(_conv3(a_ext, conv_w_ref[:, lo:hi]))
        hid_scr[:, lo:hi] = (gate * val).astype(BF16)
    out = jnp.dot(hid_scr[...], w_d_ref[...], preferred_element_type=F32)
    out_ref[0] = x_ref[0] + out * _rms_scale(out) * post_g_ref[...]


def _ffn(x, pre_g, w_up, conv_w, w_d, post_g):
    b, s, d = x.shape
    main, prev, nxt = _halo_specs(s, d)
    const2 = lambda bi, i: (0, 0)
    once = pl.Buffered(1)
    return pl.pallas_call(
        _ffn_kernel,
        out_shape=jax.ShapeDtypeStruct((b, s, d), F32),
        grid=(b, s // ROW_TILE),
        in_specs=[main, prev, nxt,
                  pl.BlockSpec(pre_g.shape, const2),
                  pl.BlockSpec(w_up.shape, const2, pipeline_mode=once),
                  pl.BlockSpec(conv_w.shape, const2),
                  pl.BlockSpec(w_d.shape, const2, pipeline_mode=once),
                  pl.BlockSpec(post_g.shape, const2)],
        out_specs=pl.BlockSpec((1, ROW_TILE, d), lambda bi, i: (bi, i, 0)),
        scratch_shapes=[
            pltpu.VMEM((ROW_TILE + 2 * HALO, d), BF16),
            pltpu.VMEM((ROW_TILE, w_d.shape[0]), BF16),
        ],
        compiler_params=pltpu.CompilerParams(
            dimension_semantics=("arbitrary", "arbitrary"),
            vmem_limit_bytes=V7X_VMEM_LIMIT_BYTES),
        name="ffn",
    )(x, x, x, pre_g, w_up, conv_w, w_d, post_g)


def _rope_tables_t(seq):
    t = np.arange(seq)
    freqs = ROPE_THETA ** (-np.arange(ROT_HALF, dtype=np.float64) / ROT_HALF)

    def axis_tables(pos):
        ang = pos.astype(np.float64)[:, None] * freqs[None, :]
        ang = np.concatenate([ang, ang], axis=-1)
        return np.cos(ang), np.sin(ang)

    cos_r, sin_r = axis_tables(t // GRID_W)
    cos_c, sin_c = axis_tables(t % GRID_W)
    sign = np.where(np.arange(AXIS_ROPE_DIM) < ROT_HALF, -1.0, 1.0)
    cos = np.concatenate([cos_r, cos_c], axis=-1).astype(np.float32)
    sin = np.concatenate([sin_r * sign, sin_c * sign], axis=-1).astype(np.float32)

    def tiled(table):
        return np.ascontiguousarray(
            table.reshape(seq // Q_ROWS, Q_ROWS, HEAD_DIM).transpose(0, 2, 1))

    return tiled(cos), tiled(sin)


def kernel(x, mix_pre_g, w_in, gate_b, q_norm_g, k_norm_g, mix_conv_w, w_attn_proj, w_conv_proj,
           w_out, mix_post_g, ffn_pre_g, w_up, ffn_conv_w, w_down, ffn_post_g):
    depth = w_in.shape[0]
    seq, d = x.shape[1], x.shape[2]
    q_w = N_HEADS * HEAD_DIM
    kv_w = N_KV_HEADS * HEAD_DIM
    d_conv = mix_conv_w.shape[2]
    cos_t, sin_t = _rope_tables_t(seq)
    for l in range(depth):
        w = w_in[l].astype(BF16)
        o0 = q_w + 2 * kv_w
        w_qkv_t = w[:, :o0].T
        w_u = w[:, o0:o0 + d_conv]
        w_b = w[:, o0 + d_conv:o0 + 2 * d_conv]
        w_c = w[:, o0 + 2 * d_conv:o0 + 3 * d_conv]
        w_ga = w[:, o0 + 3 * d_conv:o0 + 3 * d_conv + d]
        w_gb = w[:, o0 + 3 * d_conv + d:]
        qg_t = jnp.broadcast_to(q_norm_g[l][:, None], (HEAD_DIM, Q_ROWS))
        kg_t = jnp.broadcast_to(k_norm_g[l][:, None], (HEAD_DIM, Q_ROWS))
        score_bound = (HEAD_DIM * SCALE * LOG2_E * 1.01
                       * jnp.max(jnp.abs(q_norm_g[l])) * jnp.max(jnp.abs(k_norm_g[l])))
        shift = jnp.logical_not(score_bound <= EXP2_SAFE_SCORE).astype(jnp.int32).reshape(1)
        attn = _attention(shift, x, mix_pre_g[l][None], w_qkv_t, qg_t, kg_t, cos_t, sin_t)
        x = _mixer(x, attn, mix_pre_g[l][None], w_u, w_c, w_b, w_ga, w_gb, gate_b[l][None],
                   mix_conv_w[l], w_attn_proj[l].astype(BF16), w_conv_proj[l].astype(BF16),
                   w_out[l].astype(BF16), mix_post_g[l][None])
        x = _ffn(x, ffn_pre_g[l][None], w_up[l].astype(BF16), ffn_conv_w[l],
                 w_down[l].astype(BF16), ffn_post_g[l][None])
    return x
```

```python
import numpy as np

import jax
import jax.numpy as jnp
from jax import lax
from jax.experimental import pallas as pl
from jax.experimental.pallas import tpu as pltpu

N_HEADS = 8
N_KV_HEADS = 2
HEAD_DIM = 128
GROUP = N_HEADS // N_KV_HEADS
GRID_W = 64
ROPE_THETA = 10000.0
AXIS_ROPE_DIM = HEAD_DIM // 2
ROT_HALF = AXIS_ROPE_DIM // 2
EPS = 1e-6
SCALE = HEAD_DIM ** -0.5
LOG2_E = 1.4426950408889634

SUBLANES = 8
HALO = 16
ONES_ROWS = 16
V7X_VMEM_LIMIT_BYTES = 56 * 1024 * 1024

ROW_TILE = 1024
Q_ROWS = 512
PROJ_ROWS = 256
ITEMS_PER_STEP = 8
EXP2_SAFE_SCORE = 64.0
FF_CHUNK = 256

F32 = jnp.float32
BF16 = jnp.bfloat16


def _rms_scale(v):
    return lax.rsqrt(jnp.mean(v * v, axis=-1, keepdims=True) + EPS)


def _sigmoid(v):
    return 1.0 / (1.0 + jnp.exp(-v))


def _gelu_tanh(v):
    c = 0.7978845608028654
    return 0.5 * v * (1.0 + jnp.tanh(c * (v + 0.044715 * (v * v * v))))


def _attention_kernel(shift_ref, x_hbm, g_ref, wt_ref, qg_t_ref, kg_t_ref, cos_t_ref, sin_t_ref,
                      o_ref, qt_scr, k_scr, vt_scr, s_even, s_odd, m_even, m_odd, x_buf, x_sem):
    seq = x_hbm.shape[1]
    n_tiles = seq // Q_ROWS
    batch = pl.program_id(0)
    n_items = N_HEADS * n_tiles
    tile_bits = n_tiles.bit_length() - 1
    group_bits = GROUP.bit_length() - 1
    q_w = N_HEADS * HEAD_DIM
    kv_w = N_KV_HEADS * HEAD_DIM

    def rot_half(v):
        r = ROT_HALF
        return jnp.concatenate([v[r:2 * r], v[0:r], v[3 * r:4 * r], v[2 * r:3 * r]], axis=0)

    def rope_cols(zt, cos_g, sin_g, out_scale):
        inv = lax.rsqrt(jnp.mean(zt * zt, axis=0, keepdims=True) + EPS) * out_scale
        zn = zt * inv
        return zn * cos_g + rot_half(zn) * sin_g

    def x_copy(bi):
        return pltpu.make_async_copy(x_hbm.at[bi], x_buf, x_sem.at[0])

    def proj_step(r):
        rows = pl.ds(r * Q_ROWS, Q_ROWS)
        xt = x_buf[rows, :]
        h = (xt * _rms_scale(xt) * g_ref[...]).astype(BF16)
        z_t = jnp.concatenate(
            [lax.dot_general(wt_ref[g * PROJ_ROWS:(g + 1) * PROJ_ROWS, :], h,
                             (((1,), (1,)), ((), ())), preferred_element_type=F32)
             for g in range(wt_ref.shape[0] // PROJ_ROWS)], axis=0)
        cos_t = cos_t_ref[r]
        sin_t = sin_t_ref[r]
        cos_q, sin_q = qg_t_ref[...] * cos_t, rot_half(qg_t_ref[...]) * sin_t
        cos_k, sin_k = kg_t_ref[...] * cos_t, rot_half(kg_t_ref[...]) * sin_t
        for hh in range(N_HEADS):
            zq_t = z_t[hh * HEAD_DIM:(hh + 1) * HEAD_DIM]
            qt_scr[hh, r] = rope_cols(zq_t, cos_q, sin_q, SCALE * LOG2_E).astype(BF16)
        for kh in range(N_KV_HEADS):
            zk_t = z_t[q_w + kh * HEAD_DIM:q_w + (kh + 1) * HEAD_DIM]
            k_scr[kh, rows, :] = rope_cols(zk_t, cos_k, sin_k, 1.0).T.astype(BF16)
            zv_t = z_t[q_w + kv_w + kh * HEAD_DIM:q_w + kv_w + (kh + 1) * HEAD_DIM]
            vt_scr[kh, r, 0:HEAD_DIM, :] = zv_t.astype(BF16)
            vt_scr[kh, r, HEAD_DIM:, :] = jnp.ones((ONES_ROWS, Q_ROWS), BF16)

    @pl.when(batch == 0)
    def _():
        x_copy(0).start()

    x_copy(batch).wait()
    for r in range(n_tiles):
        proj_step(r)

    def split(item):
        hh = item >> tile_bits
        t = item & (n_tiles - 1)
        kh = hh >> group_bits
        return hh, t, kh

    def tile_rows(t):
        start = t * Q_ROWS
        if not isinstance(start, int):
            start = pl.multiple_of(start, Q_ROWS)
        return pl.ds(start, Q_ROWS)

    def scores(item, s_scr, m_scr):
        hh, t, kh = split(item)
        s_t = jnp.dot(k_scr[kh], qt_scr[hh, t], preferred_element_type=F32)
        s_scr[...] = s_t
        m_scr[...] = jnp.max(s_t, axis=0, keepdims=True)

    def finish(item, s_scr, m_scr):
        attend(item, jnp.exp2(s_scr[...] - m_scr[...]).astype(BF16))

    def attend(item, p16, denom=None):
        hh, t, kh = split(item)
        v_rows = HEAD_DIM + (ONES_ROWS if denom is None else 0)
        acc = jnp.dot(vt_scr[kh, 0, 0:v_rows, :], p16[0:Q_ROWS], preferred_element_type=F32)
        for kt in range(1, n_tiles):
            acc = acc + jnp.dot(vt_scr[kh, kt, 0:v_rows, :], p16[kt * Q_ROWS:(kt + 1) * Q_ROWS],
                                preferred_element_type=F32)
        if denom is None:
            denom = acc[HEAD_DIM:HEAD_DIM + 1]
        o_ref[0, hh, t] = (acc[0:HEAD_DIM] * (1.0 / denom)).astype(BF16)

    def pair_step(j, carry):
        scores(2 * j + 1, s_odd, m_odd)
        finish(2 * j, s_even, m_even)
        scores(2 * j + 2, s_even, m_even)
        finish(2 * j + 1, s_odd, m_odd)
        return carry

    def unshifted_step(j, carry):
        for i in range(ITEMS_PER_STEP):
            item = ITEMS_PER_STEP * j + i
            hh, t, kh = split(item)
            p_t = jnp.exp2(jnp.dot(k_scr[kh], qt_scr[hh, t], preferred_element_type=F32))
            attend(item, p_t.astype(BF16), jnp.sum(p_t, axis=0, keepdims=True))
        return carry

    @pl.when(batch + 1 < pl.num_programs(0))
    def _():
        x_copy(batch + 1).start()

    shift = shift_ref[0] != 0

    @pl.when(shift)
    def _():
        scores(0, s_even, m_even)
        lax.fori_loop(0, n_items // 2 - 1, pair_step, 0)
        scores(n_items - 1, s_odd, m_odd)
        finish(n_items - 2, s_even, m_even)
        finish(n_items - 1, s_odd, m_odd)

    @pl.when(jnp.logical_not(shift))
    def _():
        lax.fori_loop(0, n_items // ITEMS_PER_STEP, unshifted_step, 0)


def _attention(shift, x, pre_g, w_qkv_t, qg_t, kg_t, cos_t, sin_t):
    b, s, d = x.shape
    n_tiles = s // Q_ROWS
    const2 = lambda i: (0, 0)
    const3 = lambda i: (0, 0, 0)
    once = pl.Buffered(1)
    return pl.pallas_call(
        _attention_kernel,
        out_shape=jax.ShapeDtypeStruct((b, N_HEADS, n_tiles, HEAD_DIM, Q_ROWS), BF16),
        grid=(b,),
        in_specs=[
            pl.BlockSpec(memory_space=pltpu.SMEM),
            pl.BlockSpec(memory_space=pl.ANY),
            pl.BlockSpec((1, d), const2),
            pl.BlockSpec(w_qkv_t.shape, const2, pipeline_mode=once),
            pl.BlockSpec(qg_t.shape, const2),
            pl.BlockSpec(kg_t.shape, const2),
            pl.BlockSpec(cos_t.shape, const3, pipeline_mode=once),
            pl.BlockSpec(sin_t.shape, const3, pipeline_mode=once),
        ],
        out_specs=pl.BlockSpec((1, N_HEADS, n_tiles, HEAD_DIM, Q_ROWS),
                               lambda i: (i, 0, 0, 0, 0)),
        scratch_shapes=[
            pltpu.VMEM((N_HEADS, n_tiles, HEAD_DIM, Q_ROWS), BF16),
            pltpu.VMEM((N_KV_HEADS, s, HEAD_DIM), BF16),
            pltpu.VMEM((N_KV_HEADS, n_tiles, HEAD_DIM + ONES_ROWS, Q_ROWS), BF16),
            pltpu.VMEM((s, Q_ROWS), F32),
            pltpu.VMEM((s, Q_ROWS), F32),
            pltpu.VMEM((1, Q_ROWS), F32),
            pltpu.VMEM((1, Q_ROWS), F32),
            pltpu.VMEM((s, d), F32),
            pltpu.SemaphoreType.DMA((1,)),
        ],
        compiler_params=pltpu.CompilerParams(
            dimension_semantics=("arbitrary",), vmem_limit_bytes=V7X_VMEM_LIMIT_BYTES),
        name="attention",
    )(shift, x, pre_g, w_qkv_t, qg_t, kg_t, cos_t, sin_t)


def _halo_specs(seq, d):
    per_tile = ROW_TILE // HALO
    last = seq // HALO - 1
    main = pl.BlockSpec((1, ROW_TILE, d), lambda b, i: (b, i, 0))
    prev = pl.BlockSpec((1, HALO, d), lambda b, i: (b, jnp.maximum(i * per_tile - 1, 0), 0))
    nxt = pl.BlockSpec((1, HALO, d), lambda b, i: (b, jnp.minimum((i + 1) * per_tile, last), 0))
    return main, prev, nxt


def _normed_with_halo(h_scr, x_ref, xp_ref, xn_ref, g_ref):
    i = pl.program_id(1)
    gain = g_ref[...]

    def normed(v):
        return v * _rms_scale(v) * gain

    keep_prev = (i > 0).astype(F32)
    keep_next = (i < pl.num_programs(1) - 1).astype(F32)
    h_scr[0:HALO, :] = (normed(xp_ref[0]) * keep_prev).astype(BF16)
    h_scr[HALO:HALO + ROW_TILE, :] = normed(x_ref[0]).astype(BF16)
    h_scr[HALO + ROW_TILE:, :] = (normed(xn_ref[0]) * keep_next).astype(BF16)


def _conv3(ext, w):
    n = ROW_TILE // SUBLANES
    c = ext.shape[1]
    win = ext[HALO - SUBLANES:HALO + ROW_TILE + SUBLANES].reshape(n + 2, SUBLANES, c)
    sub = lax.broadcasted_iota(jnp.int32, (n, SUBLANES, c), 1)
    down = pltpu.roll(win, 1, 1)
    up = pltpu.roll(win, SUBLANES - 1, 1)
    prev = jnp.where(sub == 0, down[0:n], down[1:n + 1])
    nxt = jnp.where(sub == SUBLANES - 1, up[2:n + 2], up[1:n + 1])
    out = prev * w[0:1, :][None] + win[1:n + 1] * w[1:2, :][None] + nxt * w[2:3, :][None]
    return out.reshape(ROW_TILE, c)


def _mixer_kernel(x_ref, xp_ref, xn_ref, o_ref, pre_g_ref, w_u_ref, w_c_ref, w_b_ref,
                  w_ga_ref, w_gb_ref, gate_b_ref, conv_w_ref, w_attn_ref, w_conv_ref,
                  w_out_ref, post_g_ref, out_ref, h_scr):
    d = x_ref.shape[2]
    _normed_with_halo(h_scr, x_ref, xp_ref, xn_ref, pre_g_ref)
    h_ext = h_scr[...]
    u = jnp.dot(h_ext, w_u_ref[...], preferred_element_type=F32)
    c = jnp.dot(h_ext, w_c_ref[...], preferred_element_type=F32)
    h = h_scr[pl.ds(HALO, ROW_TILE), :]
    b_gate = jnp.dot(h, w_b_ref[...], preferred_element_type=F32)
    conv = (b_gate * _conv3(c * u, conv_w_ref[...])).astype(BF16)
    y_b = jnp.dot(conv, w_conv_ref[...], preferred_element_type=F32)
    attn_t = jnp.concatenate(
        [jnp.concatenate([o_ref[0, hh, tt] for tt in range(ROW_TILE // Q_ROWS)], axis=1)
         for hh in range(N_HEADS)], axis=0)
    y_a = lax.dot_general(attn_t, w_attn_ref[...], (((0,), (0,)), ((), ())),
                          preferred_element_type=F32)
    g_a = _sigmoid(jnp.dot(h, w_ga_ref[...], preferred_element_type=F32) + gate_b_ref[:, :d])
    g_b = _sigmoid(jnp.dot(h, w_gb_ref[...], preferred_element_type=F32) + gate_b_ref[:, d:])
    merged = (g_a * y_a + g_b * y_b).astype(BF16)
    out = jnp.dot(merged, w_out_ref[...], preferred_element_type=F32)
    out_ref[0] = x_ref[0] + out * _rms_scale(out) * post_g_ref[...]


def _mixer(x, o, pre_g, w_u, w_c, w_b, w_ga, w_gb, gate_b, conv_w, w_attn, w_conv, w_out, post_g):
    b, s, d = x.shape
    main, prev, nxt = _halo_specs(s, d)
    const = lambda bi, i: (0, 0)
    full = lambda a: pl.BlockSpec(a.shape, const)
    return pl.pallas_call(
        _mixer_kernel,
        out_shape=jax.ShapeDtypeStruct((b, s, d), F32),
        grid=(b, s // ROW_TILE),
        in_specs=[main, prev, nxt,
                  pl.BlockSpec((1, N_HEADS, ROW_TILE // Q_ROWS, HEAD_DIM, Q_ROWS),
                               lambda bi, i: (bi, 0, i, 0, 0)),
                  full(pre_g), full(w_u), full(w_c), full(w_b), full(w_ga), full(w_gb),
                  full(gate_b), full(conv_w), full(w_attn), full(w_conv), full(w_out),
                  full(post_g)],
        out_specs=pl.BlockSpec((1, ROW_TILE, d), lambda bi, i: (bi, i, 0)),
        scratch_shapes=[pltpu.VMEM((ROW_TILE + 2 * HALO, d), BF16)],
        compiler_params=pltpu.CompilerParams(
            dimension_semantics=("arbitrary", "arbitrary"),
            vmem_limit_bytes=V7X_VMEM_LIMIT_BYTES),
        name="mixer",
    )(x, x, x, o, pre_g, w_u, w_c, w_b, w_ga, w_gb, gate_b, conv_w, w_attn, w_conv, w_out, post_g)


def _ffn_kernel(x_ref, xp_ref, xn_ref, pre_g_ref, w_up_ref, conv_w_ref, w_d_ref, post_g_ref,
                out_ref, h_scr, hid_scr):
    d_ff = conv_w_ref.shape[1]
    _normed_with_halo(h_scr, x_ref, xp_ref, xn_ref, pre_g_ref)
    h_ext = h_scr[...]
    h = h_scr[pl.ds(HALO, ROW_TILE), :]
    for j in range(d_ff // FF_CHUNK):
        lo, hi = j * FF_CHUNK, (j + 1) * FF_CHUNK
        a_ext = jnp.dot(h_ext, w_up_ref[:, lo:hi], preferred_element_type=F32)
        val = jnp.dot(h, w_up_ref[:, d_ff + lo:d_ff + hi], preferred_element_type=F32)
        gate = _gelu_tanh(_conv3(a_ext, conv_w_ref[:, lo:hi]))
        hid_scr[:, lo:hi] = (gate * val).astype(BF16)
    out = jnp.dot(hid_scr[...], w_d_ref[...], preferred_element_type=F32)
    out_ref[0] = x_ref[0] + out * _rms_scale(out) * post_g_ref[...]


def _ffn(x, pre_g, w_up, conv_w, w_d, post_g):
    b, s, d = x.shape
    main, prev, nxt = _halo_specs(s, d)
    const2 = lambda bi, i: (0, 0)
    once = pl.Buffered(1)
    return pl.pallas_call(
        _ffn_kernel,
        out_shape=jax.ShapeDtypeStruct((b, s, d), F32),
        grid=(b, s // ROW_TILE),
        in_specs=[main, prev, nxt,
                  pl.BlockSpec(pre_g.shape, const2),
                  pl.BlockSpec(w_up.shape, const2, pipeline_mode=once),
                  pl.BlockSpec(conv_w.shape, const2),
                  pl.BlockSpec(w_d.shape, const2, pipeline_mode=once),
                  pl.BlockSpec(post_g.shape, const2)],
        out_specs=pl.BlockSpec((1, ROW_TILE, d), lambda bi, i: (bi, i, 0)),
        scratch_shapes=[
            pltpu.VMEM((ROW_TILE + 2 * HALO, d), BF16),
            pltpu.VMEM((ROW_TILE, w_d.shape[0]), BF16),
        ],
        compiler_params=pltpu.CompilerParams(
            dimension_semantics=("arbitrary", "arbitrary"),
            vmem_limit_bytes=V7X_VMEM_LIMIT_BYTES),
        name="ffn",
    )(x, x, x, pre_g, w_up, conv_w, w_d, post_g)


def _rope_tables_t(seq):
    t = np.arange(seq)
    freqs = ROPE_THETA ** (-np.arange(ROT_HALF, dtype=np.float64) / ROT_HALF)

    def axis_tables(pos):
        ang = pos.astype(np.float64)[:, None] * freqs[None, :]
        ang = np.concatenate([ang, ang], axis=-1)
        return np.cos(ang), np.sin(ang)

    cos_r, sin_r = axis_tables(t // GRID_W)
    cos_c, sin_c = axis_tables(t % GRID_W)
    sign = np.where(np.arange(AXIS_ROPE_DIM) < ROT_HALF, -1.0, 1.0)
    cos = np.concatenate([cos_r, cos_c], axis=-1).astype(np.float32)
    sin = np.concatenate([sin_r * sign, sin_c * sign], axis=-1).astype(np.float32)

    def tiled(table):
        return np.ascontiguousarray(
            table.reshape(seq // Q_ROWS, Q_ROWS, HEAD_DIM).transpose(0, 2, 1))

    return tiled(cos), tiled(sin)


def kernel(x, mix_pre_g, w_in, gate_b, q_norm_g, k_norm_g, mix_conv_w, w_attn_proj, w_conv_proj,
           w_out, mix_post_g, ffn_pre_g, w_up, ffn_conv_w, w_down, ffn_post_g):
    depth = w_in.shape[0]
    seq, d = x.shape[1], x.shape[2]
    q_w = N_HEADS * HEAD_DIM
    kv_w = N_KV_HEADS * HEAD_DIM
    d_conv = mix_conv_w.shape[2]
    cos_t, sin_t = _rope_tables_t(seq)
    for l in range(depth):
        w = w_in[l].astype(BF16)
        o0 = q_w + 2 * kv_w
        w_qkv_t = w[:, :o0].T
        w_u = w[:, o0:o0 + d_conv]
        w_b = w[:, o0 + d_conv:o0 + 2 * d_conv]
        w_c = w[:, o0 + 2 * d_conv:o0 + 3 * d_conv]
        w_ga = w[:, o0 + 3 * d_conv:o0 + 3 * d_conv + d]
        w_gb = w[:, o0 + 3 * d_conv + d:]
        qg_t = jnp.broadcast_to(q_norm_g[l][:, None], (HEAD_DIM, Q_ROWS))
        kg_t = jnp.broadcast_to(k_norm_g[l][:, None], (HEAD_DIM, Q_ROWS))
        score_bound = (HEAD_DIM * SCALE * LOG2_E * 1.01
                       * jnp.max(jnp.abs(q_norm_g[l])) * jnp.max(jnp.abs(k_norm_g[l])))
        shift = jnp.logical_not(score_bound <= EXP2_SAFE_SCORE).astype(jnp.int32).reshape(1)
        attn = _attention(shift, x, mix_pre_g[l][None], w_qkv_t, qg_t, kg_t, cos_t, sin_t)
        x = _mixer(x, attn, mix_pre_g[l][None], w_u, w_c, w_b, w_ga, w_gb, gate_b[l][None],
                   mix_conv_w[l], w_attn_proj[l].astype(BF16), w_conv_proj[l].astype(BF16),
                   w_out[l].astype(BF16), mix_post_g[l][None])
        x = _ffn(x, ffn_pre_g[l][None], w_up[l].astype(BF16), ffn_conv_w[l],
                 w_down[l].astype(BF16), ffn_post_g[l][None])
    return x
```

```python
import functools

import numpy as np

import jax
import jax.numpy as jnp
from jax import lax
from jax.experimental import pallas as pl
from jax.experimental.pallas import tpu as pltpu

N_HEADS = 8
N_KV_HEADS = 2
HEAD_DIM = 128
GROUP = N_HEADS // N_KV_HEADS
GRID_W = 64
ROPE_THETA = 10000.0
AXIS_ROPE_DIM = HEAD_DIM // 2
ROT_HALF = AXIS_ROPE_DIM // 2
EPS = 1e-6
SCALE = HEAD_DIM ** -0.5
LOG2_E = 1.4426950408889634

SUBLANES = 8
HALO = 16
ONES_ROWS = 16
V7X_VMEM_LIMIT_BYTES = 56 * 1024 * 1024

ROW_TILE = 1024
Q_ROWS = 512
PROJ_ROWS = 256
ITEMS_PER_STEP = 8
EXP2_SAFE_SCORE = 64.0
FF_CHUNK = 256

F32 = jnp.float32
BF16 = jnp.bfloat16


def _rms_scale(v):
    return lax.rsqrt(jnp.mean(v * v, axis=-1, keepdims=True) + EPS)


def _sigmoid(v):
    return 1.0 / (1.0 + jnp.exp(-v))


def _gelu_tanh(v):
    c = 0.7978845608028654
    return 0.5 * v * (1.0 + jnp.tanh(c * (v + 0.044715 * (v * v * v))))


def _attention_kernel(n_cast, shift_ref, x_hbm, g_ref, wt_ref, qg_t_ref, kg_t_ref, cos_t_ref,
                      sin_t_ref, *rest):
    w_src, rest = rest[:n_cast], rest[n_cast:]
    o_ref, rest = rest[0], rest[1:]
    w_dst, rest = rest[:n_cast], rest[n_cast:]
    qt_scr, k_scr, vt_scr, s_even, s_odd, m_even, m_odd, x_buf, x_sem = rest[:9]
    stage_f32, stage_bf16, w_sem = rest[9:9 + n_cast], rest[9 + n_cast:9 + 2 * n_cast], rest[-1]
    seq = x_hbm.shape[1]
    n_tiles = seq // Q_ROWS
    batch = pl.program_id(0)

    def cast_copies(k):
        rows, cols = stage_f32[k].shape
        span = pl.ds(pl.multiple_of(batch * rows, rows), rows)
        first_col = w_src[k].shape[1] - cols
        return (pltpu.make_async_copy(w_src[k].at[span, pl.ds(first_col, cols)], stage_f32[k],
                                      w_sem.at[0, k]),
                pltpu.make_async_copy(stage_bf16[k], w_dst[k].at[span, :], w_sem.at[1, k]))

    for k in range(n_cast):
        cast_copies(k)[0].start()
    n_items = N_HEADS * n_tiles
    tile_bits = n_tiles.bit_length() - 1
    group_bits = GROUP.bit_length() - 1
    q_w = N_HEADS * HEAD_DIM
    kv_w = N_KV_HEADS * HEAD_DIM

    def rot_half(v):
        r = ROT_HALF
        return jnp.concatenate([v[r:2 * r], v[0:r], v[3 * r:4 * r], v[2 * r:3 * r]], axis=0)

    def rope_cols(zt, cos_g, sin_g, out_scale):
        inv = lax.rsqrt(jnp.mean(zt * zt, axis=0, keepdims=True) + EPS) * out_scale
        zn = zt * inv
        return zn * cos_g + rot_half(zn) * sin_g

    def x_copy(bi):
        return pltpu.make_async_copy(x_hbm.at[bi], x_buf, x_sem.at[0])

    def proj_step(r):
        rows = pl.ds(r * Q_ROWS, Q_ROWS)
        xt = x_buf[rows, :]
        h = (xt * _rms_scale(xt) * g_ref[...]).astype(BF16)
        z_t = jnp.concatenate(
            [lax.dot_general(wt_ref[g * PROJ_ROWS:(g + 1) * PROJ_ROWS, :], h,
                             (((1,), (1,)), ((), ())), preferred_element_type=F32)
             for g in range(wt_ref.shape[0] // PROJ_ROWS)], axis=0)
        cos_t = cos_t_ref[r]
        sin_t = sin_t_ref[r]
        cos_q, sin_q = qg_t_ref[...] * cos_t, rot_half(qg_t_ref[...]) * sin_t
        cos_k, sin_k = kg_t_ref[...] * cos_t, rot_half(kg_t_ref[...]) * sin_t
        for hh in range(N_HEADS):
            zq_t = z_t[hh * HEAD_DIM:(hh + 1) * HEAD_DIM]
            qt_scr[hh, r] = rope_cols(zq_t, cos_q, sin_q, SCALE * LOG2_E).astype(BF16)
        for kh in range(N_KV_HEADS):
            zk_t = z_t[q_w + kh * HEAD_DIM:q_w + (kh + 1) * HEAD_DIM]
            k_scr[kh, rows, :] = rope_cols(zk_t, cos_k, sin_k, 1.0).T.astype(BF16)
            zv_t = z_t[q_w + kv_w + kh * HEAD_DIM:q_w + kv_w + (kh + 1) * HEAD_DIM]
            vt_scr[kh, r, 0:HEAD_DIM, :] = zv_t.astype(BF16)
            vt_scr[kh, r, HEAD_DIM:, :] = jnp.ones((ONES_ROWS, Q_ROWS), BF16)

    @pl.when(batch == 0)
    def _():
        x_copy(0).start()

    x_copy(batch).wait()
    for r in range(n_tiles):
        proj_step(r)

    for k in range(n_cast):
        load, store = cast_copies(k)
        load.wait()
        stage_bf16[k][...] = stage_f32[k][...].astype(BF16)
        store.start()

    def split(item):
        hh = item >> tile_bits
        t = item & (n_tiles - 1)
        kh = hh >> group_bits
        return hh, t, kh

    def tile_rows(t):
        start = t * Q_ROWS
        if not isinstance(start, int):
            start = pl.multiple_of(start, Q_ROWS)
        return pl.ds(start, Q_ROWS)

    def scores(item, s_scr, m_scr):
        hh, t, kh = split(item)
        s_t = jnp.dot(k_scr[kh], qt_scr[hh, t], preferred_element_type=F32)
        s_scr[...] = s_t
        m_scr[...] = jnp.max(s_t, axis=0, keepdims=True)

    def finish(item, s_scr, m_scr):
        attend(item, jnp.exp2(s_scr[...] - m_scr[...]).astype(BF16))

    def attend(item, p16, denom=None):
        hh, t, kh = split(item)
        v_rows = HEAD_DIM + (ONES_ROWS if denom is None else 0)
        acc = jnp.dot(vt_scr[kh, 0, 0:v_rows, :], p16[0:Q_ROWS], preferred_element_type=F32)
        for kt in range(1, n_tiles):
            acc = acc + jnp.dot(vt_scr[kh, kt, 0:v_rows, :], p16[kt * Q_ROWS:(kt + 1) * Q_ROWS],
                                preferred_element_type=F32)
        if denom is None:
            denom = acc[HEAD_DIM:HEAD_DIM + 1]
        o_ref[0, hh, t] = (acc[0:HEAD_DIM] * (1.0 / denom)).astype(BF16)

    def pair_step(j, carry):
        scores(2 * j + 1, s_odd, m_odd)
        finish(2 * j, s_even, m_even)
        scores(2 * j + 2, s_even, m_even)
        finish(2 * j + 1, s_odd, m_odd)
        return carry

    def unshifted_step(j, carry):
        for i in range(ITEMS_PER_STEP):
            item = ITEMS_PER_STEP * j + i
            hh, t, kh = split(item)
            p_t = jnp.exp2(jnp.dot(k_scr[kh], qt_scr[hh, t], preferred_element_type=F32))
            attend(item, p_t.astype(BF16), jnp.sum(p_t, axis=0, keepdims=True))
        return carry

    @pl.when(batch + 1 < pl.num_programs(0))
    def _():
        x_copy(batch + 1).start()

    shift = shift_ref[0] != 0

    @pl.when(shift)
    def _():
        scores(0, s_even, m_even)
        lax.fori_loop(0, n_items // 2 - 1, pair_step, 0)
        scores(n_items - 1, s_odd, m_odd)
        finish(n_items - 2, s_even, m_even)
        finish(n_items - 1, s_odd, m_odd)

    @pl.when(jnp.logical_not(shift))
    def _():
        lax.fori_loop(0, n_items // ITEMS_PER_STEP, unshifted_step, 0)

    for k in range(n_cast):
        cast_copies(k)[1].wait()


def _attention(shift, x, pre_g, w_qkv_t, qg_t, kg_t, cos_t, sin_t, cast_weights):
    b, s, d = x.shape
    n_tiles = s // Q_ROWS
    const2 = lambda i: (0, 0)
    const3 = lambda i: (0, 0, 0)
    once = pl.Buffered(1)
    hbm = pl.BlockSpec(memory_space=pl.ANY)
    n_cast = len(cast_weights)
    slabs = []
    for w, n_cols in cast_weights:
        rows, rem = divmod(w.shape[0], b)
        assert rem == 0 and rows % HALO == 0
        slabs.append((rows, n_cols))
    return pl.pallas_call(
        functools.partial(_attention_kernel, n_cast),
        out_shape=[jax.ShapeDtypeStruct((b, N_HEADS, n_tiles, HEAD_DIM, Q_ROWS), BF16)]
        + [jax.ShapeDtypeStruct((w.shape[0], n_cols), BF16) for w, n_cols in cast_weights],
        grid=(b,),
        in_specs=[
            pl.BlockSpec(memory_space=pltpu.SMEM),
            pl.BlockSpec(memory_space=pl.ANY),
            pl.BlockSpec((1, d), const2),
            pl.BlockSpec(w_qkv_t.shape, const2, pipeline_mode=once),
            pl.BlockSpec(qg_t.shape, const2),
            pl.BlockSpec(kg_t.shape, const2),
            pl.BlockSpec(cos_t.shape, const3, pipeline_mode=once),
            pl.BlockSpec(sin_t.shape, const3, pipeline_mode=once),
        ] + [hbm] * n_cast,
        out_specs=[pl.BlockSpec((1, N_HEADS, n_tiles, HEAD_DIM, Q_ROWS),
                                lambda i: (i, 0, 0, 0, 0))] + [hbm] * n_cast,
        scratch_shapes=[
            pltpu.VMEM((N_HEADS, n_tiles, HEAD_DIM, Q_ROWS), BF16),
            pltpu.VMEM((N_KV_HEADS, s, HEAD_DIM), BF16),
            pltpu.VMEM((N_KV_HEADS, n_tiles, HEAD_DIM + ONES_ROWS, Q_ROWS), BF16),
            pltpu.VMEM((s, Q_ROWS), F32),
            pltpu.VMEM((s, Q_ROWS), F32),
            pltpu.VMEM((1, Q_ROWS), F32),
            pltpu.VMEM((1, Q_ROWS), F32),
            pltpu.VMEM((s, d), F32),
            pltpu.SemaphoreType.DMA((1,)),
        ] + [pltpu.VMEM(slab, F32) for slab in slabs] + [pltpu.VMEM(slab, BF16) for slab in slabs]
        + [pltpu.SemaphoreType.DMA((2, n_cast))],
        compiler_params=pltpu.CompilerParams(
            dimension_semantics=("arbitrary",), vmem_limit_bytes=V7X_VMEM_LIMIT_BYTES),
        name="attention",
    )(shift, x, pre_g, w_qkv_t, qg_t, kg_t, cos_t, sin_t, *[w for w, _ in cast_weights])


def _halo_specs(seq, d):
    per_tile = ROW_TILE // HALO
    last = seq // HALO - 1
    main = pl.BlockSpec((1, ROW_TILE, d), lambda b, i: (b, i, 0))
    prev = pl.BlockSpec((1, HALO, d), lambda b, i: (b, jnp.maximum(i * per_tile - 1, 0), 0))
    nxt = pl.BlockSpec((1, HALO, d), lambda b, i: (b, jnp.minimum((i + 1) * per_tile, last), 0))
    return main, prev, nxt


def _normed_with_halo(h_scr, x_ref, xp_ref, xn_ref, g_ref):
    i = pl.program_id(1)
    gain = g_ref[...]

    def normed(v):
        return v * _rms_scale(v) * gain

    keep_prev = (i > 0).astype(F32)
    keep_next = (i < pl.num_programs(1) - 1).astype(F32)
    h_scr[0:HALO, :] = (normed(xp_ref[0]) * keep_prev).astype(BF16)
    h_scr[HALO:HALO + ROW_TILE, :] = normed(x_ref[0]).astype(BF16)
    h_scr[HALO + ROW_TILE:, :] = (normed(xn_ref[0]) * keep_next).astype(BF16)


def _conv3(ext, w):
    n = ROW_TILE // SUBLANES
    c = ext.shape[1]
    win = ext[HALO - SUBLANES:HALO + ROW_TILE + SUBLANES].reshape(n + 2, SUBLANES, c)
    sub = lax.broadcasted_iota(jnp.int32, (n, SUBLANES, c), 1)
    down = pltpu.roll(win, 1, 1)
    up = pltpu.roll(win, SUBLANES - 1, 1)
    prev = jnp.where(sub == 0, down[0:n], down[1:n + 1])
    nxt = jnp.where(sub == SUBLANES - 1, up[2:n + 2], up[1:n + 1])
    out = prev * w[0:1, :][None] + win[1:n + 1] * w[1:2, :][None] + nxt * w[2:3, :][None]
    return out.reshape(ROW_TILE, c)


def _mixer_kernel(x_ref, xp_ref, xn_ref, o_ref, pre_g_ref, w_in_ref, gate_b_ref, conv_w_ref,
                  w_attn_ref, w_conv_ref, w_out_ref, post_g_ref, out_ref, h_scr):
    d = x_ref.shape[2]
    w_u, w_b, w_c, w_ga, w_gb = (w_in_ref.at[:, pl.ds(k * d, d)] for k in range(5))
    _normed_with_halo(h_scr, x_ref, xp_ref, xn_ref, pre_g_ref)
    h_ext = h_scr[...]
    u = jnp.dot(h_ext, w_u[...], preferred_element_type=F32)
    c = jnp.dot(h_ext, w_c[...], preferred_element_type=F32)
    h = h_scr[pl.ds(HALO, ROW_TILE), :]
    b_gate = jnp.dot(h, w_b[...], preferred_element_type=F32)
    conv = (b_gate * _conv3(c * u, conv_w_ref[...])).astype(BF16)
    y_b = jnp.dot(conv, w_conv_ref[...], preferred_element_type=F32)
    attn_t = jnp.concatenate(
        [jnp.concatenate([o_ref[0, hh, tt] for tt in range(ROW_TILE // Q_ROWS)], axis=1)
         for hh in range(N_HEADS)], axis=0)
    y_a = lax.dot_general(attn_t, w_attn_ref[...], (((0,), (0,)), ((), ())),
                          preferred_element_type=F32)
    g_a = _sigmoid(jnp.dot(h, w_ga[...], preferred_element_type=F32) + gate_b_ref[:, :d])
    g_b = _sigmoid(jnp.dot(h, w_gb[...], preferred_element_type=F32) + gate_b_ref[:, d:])
    merged = (g_a * y_a + g_b * y_b).astype(BF16)
    out = jnp.dot(merged, w_out_ref[...], preferred_element_type=F32)
    out_ref[0] = x_ref[0] + out * _rms_scale(out) * post_g_ref[...]


def _mixer(x, o, pre_g, w_in5, gate_b, conv_w, w_attn, w_conv, w_out, post_g):
    b, s, d = x.shape
    main, prev, nxt = _halo_specs(s, d)
    const = lambda bi, i: (0, 0)
    full = lambda a: pl.BlockSpec(a.shape, const)
    return pl.pallas_call(
        _mixer_kernel,
        out_shape=jax.ShapeDtypeStruct((b, s, d), F32),
        grid=(b, s // ROW_TILE),
        in_specs=[main, prev, nxt,
                  pl.BlockSpec((1, N_HEADS, ROW_TILE // Q_ROWS, HEAD_DIM, Q_ROWS),
                               lambda bi, i: (bi, 0, i, 0, 0)),
                  full(pre_g), full(w_in5), full(gate_b), full(conv_w), full(w_attn),
                  full(w_conv), full(w_out), full(post_g)],
        out_specs=pl.BlockSpec((1, ROW_TILE, d), lambda bi, i: (bi, i, 0)),
        scratch_shapes=[pltpu.VMEM((ROW_TILE + 2 * HALO, d), BF16)],
        compiler_params=pltpu.CompilerParams(
            dimension_semantics=("arbitrary", "arbitrary"),
            vmem_limit_bytes=V7X_VMEM_LIMIT_BYTES),
        name="mixer",
    )(x, x, x, o, pre_g, w_in5, gate_b, conv_w, w_attn, w_conv, w_out, post_g)


def _ffn_kernel(x_ref, xp_ref, xn_ref, pre_g_ref, w_up_ref, conv_w_ref, w_d_ref, post_g_ref,
                out_ref, h_scr, hid_scr):
    d_ff = conv_w_ref.shape[1]
    _normed_with_halo(h_scr, x_ref, xp_ref, xn_ref, pre_g_ref)
    h_ext = h_scr[...]
    h = h_scr[pl.ds(HALO, ROW_TILE), :]
    for j in range(d_ff // FF_CHUNK):
        lo, hi = j * FF_CHUNK, (j + 1) * FF_CHUNK
        a_ext = jnp.dot(h_ext, w_up_ref[:, lo:hi], preferred_element_type=F32)
        val = jnp.dot(h, w_up_ref[:, d_ff + lo:d_ff + hi], preferred_element_type=F32)
        gate = _gelu_tanh(_conv3(a_ext, conv_w_ref[:, lo:hi]))
        hid_scr[:, lo:hi] = (gate * val).astype(BF16)
    out = jnp.dot(hid_scr[...], w_d_ref[...], preferred_element_type=F32)
    out_ref[0] = x_ref[0] + out * _rms_scale(out) * post_g_ref[...]


def _ffn(x, pre_g, w_up, conv_w, w_d, post_g):
    b, s, d = x.shape
    main, prev, nxt = _halo_specs(s, d)
    const2 = lambda bi, i: (0, 0)
    once = pl.Buffered(1)
    return pl.pallas_call(
        _ffn_kernel,
        out_shape=jax.ShapeDtypeStruct((b, s, d), F32),
        grid=(b, s // ROW_TILE),
        in_specs=[main, prev, nxt,
                  pl.BlockSpec(pre_g.shape, const2),
                  pl.BlockSpec(w_up.shape, const2, pipeline_mode=once),
                  pl.BlockSpec(conv_w.shape, const2),
                  pl.BlockSpec(w_d.shape, const2, pipeline_mode=once),
                  pl.BlockSpec(post_g.shape, const2)],
        out_specs=pl.BlockSpec((1, ROW_TILE, d), lambda bi, i: (bi, i, 0)),
        scratch_shapes=[
            pltpu.VMEM((ROW_TILE + 2 * HALO, d), BF16),
            pltpu.VMEM((ROW_TILE, w_d.shape[0]), BF16),
        ],
        compiler_params=pltpu.CompilerParams(
            dimension_semantics=("arbitrary", "arbitrary"),
            vmem_limit_bytes=V7X_VMEM_LIMIT_BYTES),
        name="ffn",
    )(x, x, x, pre_g, w_up, conv_w, w_d, post_g)


def _rope_tables_t(seq):
    t = np.arange(seq)
    freqs = ROPE_THETA ** (-np.arange(ROT_HALF, dtype=np.float64) / ROT_HALF)

    def axis_tables(pos):
        ang = pos.astype(np.float64)[:, None] * freqs[None, :]
        ang = np.concatenate([ang, ang], axis=-1)
        return np.cos(ang), np.sin(ang)

    cos_r, sin_r = axis_tables(t // GRID_W)
    cos_c, sin_c = axis_tables(t % GRID_W)
    sign = np.where(np.arange(AXIS_ROPE_DIM) < ROT_HALF, -1.0, 1.0)
    cos = np.concatenate([cos_r, cos_c], axis=-1).astype(np.float32)
    sin = np.concatenate([sin_r * sign, sin_c * sign], axis=-1).astype(np.float32)

    def tiled(table):
        return np.ascontiguousarray(
            table.reshape(seq // Q_ROWS, Q_ROWS, HEAD_DIM).transpose(0, 2, 1))

    return tiled(cos), tiled(sin)


def kernel(x, mix_pre_g, w_in, gate_b, q_norm_g, k_norm_g, mix_conv_w, w_attn_proj, w_conv_proj,
           w_out, mix_post_g, ffn_pre_g, w_up, ffn_conv_w, w_down, ffn_post_g):
    depth = w_in.shape[0]
    seq, d = x.shape[1], x.shape[2]
    q_w = N_HEADS * HEAD_DIM
    kv_w = N_KV_HEADS * HEAD_DIM
    d_conv = mix_conv_w.shape[2]
    cos_t, sin_t = _rope_tables_t(seq)
    o0 = q_w + 2 * kv_w
    assert d_conv == d and w_in.shape[2] == o0 + 5 * d
    for l in range(depth):
        w_qkv_t = w_in[l][:, :o0].T.astype(BF16)
        qg_t = jnp.broadcast_to(q_norm_g[l][:, None], (HEAD_DIM, Q_ROWS))
        kg_t = jnp.broadcast_to(k_norm_g[l][:, None], (HEAD_DIM, Q_ROWS))
        score_bound = (HEAD_DIM * SCALE * LOG2_E * 1.01
                       * jnp.max(jnp.abs(q_norm_g[l])) * jnp.max(jnp.abs(k_norm_g[l])))
        shift = jnp.logical_not(score_bound <= EXP2_SAFE_SCORE).astype(jnp.int32).reshape(1)
        later_weights = [(w_in[l], 5 * d), (w_attn_proj[l], d), (w_conv_proj[l], d), (w_out[l], d),
                         (w_up[l], w_up.shape[2]), (w_down[l], d)]
        attn, w_in5, w_attn16, w_conv16, w_out16, w_up16, w_down16 = _attention(
            shift, x, mix_pre_g[l][None], w_qkv_t, qg_t, kg_t, cos_t, sin_t, later_weights)
        x = _mixer(x, attn, mix_pre_g[l][None], w_in5, gate_b[l][None], mix_conv_w[l],
                   w_attn16, w_conv16, w_out16, mix_post_g[l][None])
        x = _ffn(x, ffn_pre_g[l][None], w_up16, ffn_conv_w[l], w_down16, ffn_post_g[l][None])
    return x
```

```python
import functools

import numpy as np

import jax
import jax.numpy as jnp
from jax import lax
from jax.experimental import pallas as pl
from jax.experimental.pallas import tpu as pltpu

N_HEADS = 8
N_KV_HEADS = 2
HEAD_DIM = 128
GROUP = N_HEADS // N_KV_HEADS
GRID_W = 64
ROPE_THETA = 10000.0
AXIS_ROPE_DIM = HEAD_DIM // 2
ROT_HALF = AXIS_ROPE_DIM // 2
EPS = 1e-6
SCALE = HEAD_DIM ** -0.5
LOG2_E = 1.4426950408889634

SUBLANES = 8
HALO = 16
ONES_ROWS = 16
V7X_VMEM_LIMIT_BYTES = 56 * 1024 * 1024

ROW_TILE = 1024
Q_ROWS = 512
PROJ_ROWS = 256
ITEMS_PER_STEP = 8
EXP2_SAFE_SCORE = 64.0
FF_CHUNK = 256

F32 = jnp.float32
BF16 = jnp.bfloat16


def _rms_scale(v):
    return lax.rsqrt(jnp.mean(v * v, axis=-1, keepdims=True) + EPS)


def _sigmoid(v):
    return 1.0 / (1.0 + jnp.exp(-v))


def _gelu_tanh(v):
    c = 0.7978845608028654
    return 0.5 * v * (1.0 + jnp.tanh(c * (v + 0.044715 * (v * v * v))))


def _attention_kernel(n_cast, shift_ref, x_hbm, g_ref, wt_ref, qg_t_ref, kg_t_ref, cos_t_ref,
                      sin_t_ref, *rest):
    w_src, rest = rest[:n_cast], rest[n_cast:]
    o_ref, rest = rest[0], rest[1:]
    w_dst, rest = rest[:n_cast], rest[n_cast:]
    qt_scr, k_scr, vt_scr, s_even, s_odd, m_even, m_odd, x_buf, x_sem = rest[:9]
    stage_f32, stage_bf16, w_sem = rest[9:9 + n_cast], rest[9 + n_cast:9 + 2 * n_cast], rest[-1]
    seq = x_hbm.shape[1]
    n_tiles = seq // Q_ROWS
    batch = pl.program_id(0)

    def cast_copies(k):
        rows, cols = stage_f32[k].shape
        span = pl.ds(pl.multiple_of(batch * rows, rows), rows)
        first_col = w_src[k].shape[1] - cols
        return (pltpu.make_async_copy(w_src[k].at[span, pl.ds(first_col, cols)], stage_f32[k],
                                      w_sem.at[0, k]),
                pltpu.make_async_copy(stage_bf16[k], w_dst[k].at[span, :], w_sem.at[1, k]))

    for k in range(n_cast):
        cast_copies(k)[0].start()
    n_items = N_HEADS * n_tiles
    tile_bits = n_tiles.bit_length() - 1
    group_bits = GROUP.bit_length() - 1
    q_w = N_HEADS * HEAD_DIM
    kv_w = N_KV_HEADS * HEAD_DIM

    def rot_half(v):
        r = ROT_HALF
        return jnp.concatenate([v[r:2 * r], v[0:r], v[3 * r:4 * r], v[2 * r:3 * r]], axis=0)

    def rope_cols(zt, cos_g, sin_g, out_scale):
        inv = lax.rsqrt(jnp.mean(zt * zt, axis=0, keepdims=True) + EPS) * out_scale
        zn = zt * inv
        return zn * cos_g + rot_half(zn) * sin_g

    def x_copy(bi):
        return pltpu.make_async_copy(x_hbm.at[bi], x_buf, x_sem.at[0])

    def proj_step(r):
        rows = pl.ds(r * Q_ROWS, Q_ROWS)
        xt = x_buf[rows, :]
        h = (xt * _rms_scale(xt) * g_ref[...]).astype(BF16)
        z_t = jnp.concatenate(
            [lax.dot_general(wt_ref[g * PROJ_ROWS:(g + 1) * PROJ_ROWS, :], h,
                             (((1,), (1,)), ((), ())), preferred_element_type=F32)
             for g in range(wt_ref.shape[0] // PROJ_ROWS)], axis=0)
        cos_t = cos_t_ref[r]
        sin_t = sin_t_ref[r]
        cos_q, sin_q = qg_t_ref[...] * cos_t, rot_half(qg_t_ref[...]) * sin_t
        cos_k, sin_k = kg_t_ref[...] * cos_t, rot_half(kg_t_ref[...]) * sin_t
        for hh in range(N_HEADS):
            zq_t = z_t[hh * HEAD_DIM:(hh + 1) * HEAD_DIM]
            qt_scr[hh, r] = rope_cols(zq_t, cos_q, sin_q, SCALE * LOG2_E).astype(BF16)
        for kh in range(N_KV_HEADS):
            zk_t = z_t[q_w + kh * HEAD_DIM:q_w + (kh + 1) * HEAD_DIM]
            k_scr[kh, rows, :] = rope_cols(zk_t, cos_k, sin_k, 1.0).T.astype(BF16)
            zv_t = z_t[q_w + kv_w + kh * HEAD_DIM:q_w + kv_w + (kh + 1) * HEAD_DIM]
            vt_scr[kh, r, 0:HEAD_DIM, :] = zv_t.astype(BF16)
            vt_scr[kh, r, HEAD_DIM:, :] = jnp.ones((ONES_ROWS, Q_ROWS), BF16)

    @pl.when(batch == 0)
    def _():
        x_copy(0).start()

    x_copy(batch).wait()
    for r in range(n_tiles):
        proj_step(r)

    for k in range(n_cast):
        load, store = cast_copies(k)
        load.wait()
        stage_bf16[k][...] = stage_f32[k][...].astype(BF16)
        store.start()

    def split(item):
        hh = item >> tile_bits
        t = item & (n_tiles - 1)
        kh = hh >> group_bits
        return hh, t, kh

    def tile_rows(t):
        start = t * Q_ROWS
        if not isinstance(start, int):
            start = pl.multiple_of(start, Q_ROWS)
        return pl.ds(start, Q_ROWS)

    def scores(item, s_scr, m_scr):
        hh, t, kh = split(item)
        s_t = jnp.dot(k_scr[kh], qt_scr[hh, t], preferred_element_type=F32)
        s_scr[...] = s_t
        m_scr[...] = jnp.max(s_t, axis=0, keepdims=True)

    def finish(item, s_scr, m_scr):
        attend(item, jnp.exp2(s_scr[...] - m_scr[...]).astype(BF16))

    def attend(item, p16, denom=None):
        hh, t, kh = split(item)
        v_rows = HEAD_DIM + (ONES_ROWS if denom is None else 0)
        acc = jnp.dot(vt_scr[kh, 0, 0:v_rows, :], p16[0:Q_ROWS], preferred_element_type=F32)
        for kt in range(1, n_tiles):
            acc = acc + jnp.dot(vt_scr[kh, kt, 0:v_rows, :], p16[kt * Q_ROWS:(kt + 1) * Q_ROWS],
                                preferred_element_type=F32)
        if denom is None:
            denom = acc[HEAD_DIM:HEAD_DIM + 1]
        o_ref[0, hh, t] = (acc[0:HEAD_DIM] * (1.0 / denom)).astype(BF16)

    def pair_step(j, carry):
        scores(2 * j + 1, s_odd, m_odd)
        finish(2 * j, s_even, m_even)
        scores(2 * j + 2, s_even, m_even)
        finish(2 * j + 1, s_odd, m_odd)
        return carry

    def unshifted_step(j, carry):
        for i in range(ITEMS_PER_STEP):
            item = ITEMS_PER_STEP * j + i
            hh, t, kh = split(item)
            p_t = jnp.exp2(jnp.dot(k_scr[kh], qt_scr[hh, t], preferred_element_type=F32))
            attend(item, p_t.astype(BF16), jnp.sum(p_t, axis=0, keepdims=True))
        return carry

    @pl.when(batch + 1 < pl.num_programs(0))
    def _():
        x_copy(batch + 1).start()

    shift = shift_ref[0] != 0

    @pl.when(shift)
    def _():
        scores(0, s_even, m_even)
        lax.fori_loop(0, n_items // 2 - 1, pair_step, 0)
        scores(n_items - 1, s_odd, m_odd)
        finish(n_items - 2, s_even, m_even)
        finish(n_items - 1, s_odd, m_odd)

    @pl.when(jnp.logical_not(shift))
    def _():
        lax.fori_loop(0, n_items // ITEMS_PER_STEP, unshifted_step, 0)

    for k in range(n_cast):
        cast_copies(k)[1].wait()


def _attention(shift, x, pre_g, w_qkv_t, qg_t, kg_t, cos_t, sin_t, cast_weights):
    b, s, d = x.shape
    n_tiles = s // Q_ROWS
    const2 = lambda i: (0, 0)
    const3 = lambda i: (0, 0, 0)
    once = pl.Buffered(1)
    hbm = pl.BlockSpec(memory_space=pl.ANY)
    n_cast = len(cast_weights)
    slabs = []
    for w, n_cols in cast_weights:
        rows, rem = divmod(w.shape[0], b)
        assert rem == 0 and rows % HALO == 0
        slabs.append((rows, n_cols))
    return pl.pallas_call(
        functools.partial(_attention_kernel, n_cast),
        out_shape=[jax.ShapeDtypeStruct((b, N_HEADS, n_tiles, HEAD_DIM, Q_ROWS), BF16)]
        + [jax.ShapeDtypeStruct((w.shape[0], n_cols), BF16) for w, n_cols in cast_weights],
        grid=(b,),
        in_specs=[
            pl.BlockSpec(memory_space=pltpu.SMEM),
            pl.BlockSpec(memory_space=pl.ANY),
            pl.BlockSpec((1, d), const2),
            pl.BlockSpec(w_qkv_t.shape, const2, pipeline_mode=once),
            pl.BlockSpec(qg_t.shape, const2),
            pl.BlockSpec(kg_t.shape, const2),
            pl.BlockSpec(cos_t.shape, const3, pipeline_mode=once),
            pl.BlockSpec(sin_t.shape, const3, pipeline_mode=once),
        ] + [hbm] * n_cast,
        out_specs=[pl.BlockSpec((1, N_HEADS, n_tiles, HEAD_DIM, Q_ROWS),
                                lambda i: (i, 0, 0, 0, 0))] + [hbm] * n_cast,
        scratch_shapes=[
            pltpu.VMEM((N_HEADS, n_tiles, HEAD_DIM, Q_ROWS), BF16),
            pltpu.VMEM((N_KV_HEADS, s, HEAD_DIM), BF16),
            pltpu.VMEM((N_KV_HEADS, n_tiles, HEAD_DIM + ONES_ROWS, Q_ROWS), BF16),
            pltpu.VMEM((s, Q_ROWS), F32),
            pltpu.VMEM((s, Q_ROWS), F32),
            pltpu.VMEM((1, Q_ROWS), F32),
            pltpu.VMEM((1, Q_ROWS), F32),
            pltpu.VMEM((s, d), F32),
            pltpu.SemaphoreType.DMA((1,)),
        ] + [pltpu.VMEM(slab, F32) for slab in slabs] + [pltpu.VMEM(slab, BF16) for slab in slabs]
        + [pltpu.SemaphoreType.DMA((2, n_cast))],
        compiler_params=pltpu.CompilerParams(
            dimension_semantics=("arbitrary",), vmem_limit_bytes=V7X_VMEM_LIMIT_BYTES),
        name="attention",
    )(shift, x, pre_g, w_qkv_t, qg_t, kg_t, cos_t, sin_t, *[w for w, _ in cast_weights])


def _halo_specs(seq, d):
    per_tile = ROW_TILE // HALO
    last = seq // HALO - 1
    main = pl.BlockSpec((1, ROW_TILE, d), lambda b, i: (b, i, 0))
    prev = pl.BlockSpec((1, HALO, d), lambda b, i: (b, jnp.maximum(i * per_tile - 1, 0), 0))
    nxt = pl.BlockSpec((1, HALO, d), lambda b, i: (b, jnp.minimum((i + 1) * per_tile, last), 0))
    return main, prev, nxt


def _normed_with_halo(h_scr, x_ref, xp_ref, xn_ref, g_ref):
    i = pl.program_id(1)
    gain = g_ref[...]

    def normed(v):
        return v * _rms_scale(v) * gain

    keep_prev = (i > 0).astype(F32)
    keep_next = (i < pl.num_programs(1) - 1).astype(F32)
    h_scr[0:HALO, :] = (normed(xp_ref[0]) * keep_prev).astype(BF16)
    h_scr[HALO:HALO + ROW_TILE, :] = normed(x_ref[0]).astype(BF16)
    h_scr[HALO + ROW_TILE:, :] = (normed(xn_ref[0]) * keep_next).astype(BF16)


def _conv3(ext, w):
    n = ROW_TILE // SUBLANES
    c = ext.shape[1]
    win = ext[HALO - SUBLANES:HALO + ROW_TILE + SUBLANES].reshape(n + 2, SUBLANES, c)
    sub = lax.broadcasted_iota(jnp.int32, (n, SUBLANES, c), 1)
    down = pltpu.roll(win, 1, 1)
    up = pltpu.roll(win, SUBLANES - 1, 1)
    prev = jnp.where(sub == 0, down[0:n], down[1:n + 1])
    nxt = jnp.where(sub == SUBLANES - 1, up[2:n + 2], up[1:n + 1])
    out = prev * w[0:1, :][None] + win[1:n + 1] * w[1:2, :][None] + nxt * w[2:3, :][None]
    return out.reshape(ROW_TILE, c)


def _mixer_kernel(x_ref, xp_ref, xn_ref, o_ref, pre_g_ref, w_in_ref, gate_b_ref, conv_w_ref,
                  w_attn_ref, w_conv_ref, w_out_ref, post_g_ref, next_g_ref, out_ref, next_h_ref,
                  h_scr):
    d = x_ref.shape[2]
    w_u, w_b, w_c, w_ga, w_gb = (w_in_ref.at[:, pl.ds(k * d, d)] for k in range(5))
    _normed_with_halo(h_scr, x_ref, xp_ref, xn_ref, pre_g_ref)
    h_ext = h_scr[...]
    u = jnp.dot(h_ext, w_u[...], preferred_element_type=F32)
    c = jnp.dot(h_ext, w_c[...], preferred_element_type=F32)
    h = h_scr[pl.ds(HALO, ROW_TILE), :]
    b_gate = jnp.dot(h, w_b[...], preferred_element_type=F32)
    conv = (b_gate * _conv3(c * u, conv_w_ref[...])).astype(BF16)
    y_b = jnp.dot(conv, w_conv_ref[...], preferred_element_type=F32)
    attn_t = jnp.concatenate(
        [jnp.concatenate([o_ref[0, hh, tt] for tt in range(ROW_TILE // Q_ROWS)], axis=1)
         for hh in range(N_HEADS)], axis=0)
    y_a = lax.dot_general(attn_t, w_attn_ref[...], (((0,), (0,)), ((), ())),
                          preferred_element_type=F32)
    g_a = _sigmoid(jnp.dot(h, w_ga[...], preferred_element_type=F32) + gate_b_ref[:, :d])
    g_b = _sigmoid(jnp.dot(h, w_gb[...], preferred_element_type=F32) + gate_b_ref[:, d:])
    merged = (g_a * y_a + g_b * y_b).astype(BF16)
    out = jnp.dot(merged, w_out_ref[...], preferred_element_type=F32)
    x1 = x_ref[0] + out * _rms_scale(out) * post_g_ref[...]
    out_ref[0] = x1
    next_h_ref[0] = (x1 * _rms_scale(x1) * next_g_ref[...]).astype(BF16)


def _mixer(x, o, pre_g, w_in5, gate_b, conv_w, w_attn, w_conv, w_out, post_g, next_g):
    b, s, d = x.shape
    main, prev, nxt = _halo_specs(s, d)
    const = lambda bi, i: (0, 0)
    full = lambda a: pl.BlockSpec(a.shape, const)
    return pl.pallas_call(
        _mixer_kernel,
        out_shape=[jax.ShapeDtypeStruct((b, s, d), F32), jax.ShapeDtypeStruct((b, s, d), BF16)],
        grid=(b, s // ROW_TILE),
        in_specs=[main, prev, nxt,
                  pl.BlockSpec((1, N_HEADS, ROW_TILE // Q_ROWS, HEAD_DIM, Q_ROWS),
                               lambda bi, i: (bi, 0, i, 0, 0)),
                  full(pre_g), full(w_in5), full(gate_b), full(conv_w), full(w_attn),
                  full(w_conv), full(w_out), full(post_g), full(next_g)],
        out_specs=[main, main],
        scratch_shapes=[pltpu.VMEM((ROW_TILE + 2 * HALO, d), BF16)],
        compiler_params=pltpu.CompilerParams(
            dimension_semantics=("arbitrary", "arbitrary"),
            vmem_limit_bytes=V7X_VMEM_LIMIT_BYTES),
        name="mixer",
    )(x, x, x, o, pre_g, w_in5, gate_b, conv_w, w_attn, w_conv, w_out, post_g, next_g)


def _ffn_kernel(x_ref, h_ref, hp_ref, hn_ref, w_up_ref, conv_w_ref, w_d_ref, post_g_ref,
                out_ref, h_scr, hid_scr):
    d_ff = conv_w_ref.shape[1]
    i = pl.program_id(1)
    zeros = jnp.zeros((HALO, h_ref.shape[2]), BF16)
    h_scr[0:HALO, :] = jnp.where(i > 0, hp_ref[0], zeros)
    h_scr[HALO:HALO + ROW_TILE, :] = h_ref[0]
    h_scr[HALO + ROW_TILE:, :] = jnp.where(i < pl.num_programs(1) - 1, hn_ref[0], zeros)
    h_ext = h_scr[...]
    h = h_scr[pl.ds(HALO, ROW_TILE), :]
    for j in range(d_ff // FF_CHUNK):
        lo, hi = j * FF_CHUNK, (j + 1) * FF_CHUNK
        a_ext = jnp.dot(h_ext, w_up_ref[:, lo:hi], preferred_element_type=F32)
        val = jnp.dot(h, w_up_ref[:, d_ff + lo:d_ff + hi], preferred_element_type=F32)
        gate = _gelu_tanh(_conv3(a_ext, conv_w_ref[:, lo:hi]))
        hid_scr[:, lo:hi] = (gate * val).astype(BF16)
    out = jnp.dot(hid_scr[...], w_d_ref[...], preferred_element_type=F32)
    out_ref[0] = x_ref[0] + out * _rms_scale(out) * post_g_ref[...]


def _ffn(x, h, w_up, conv_w, w_d, post_g):
    b, s, d = x.shape
    main, prev, nxt = _halo_specs(s, d)
    const2 = lambda bi, i: (0, 0)
    once = pl.Buffered(1)
    return pl.pallas_call(
        _ffn_kernel,
        out_shape=jax.ShapeDtypeStruct((b, s, d), F32),
        grid=(b, s // ROW_TILE),
        in_specs=[main, main, prev, nxt,
                  pl.BlockSpec(w_up.shape, const2, pipeline_mode=once),
                  pl.BlockSpec(conv_w.shape, const2),
                  pl.BlockSpec(w_d.shape, const2, pipeline_mode=once),
                  pl.BlockSpec(post_g.shape, const2)],
        out_specs=pl.BlockSpec((1, ROW_TILE, d), lambda bi, i: (bi, i, 0)),
        scratch_shapes=[
            pltpu.VMEM((ROW_TILE + 2 * HALO, d), BF16),
            pltpu.VMEM((ROW_TILE, w_d.shape[0]), BF16),
        ],
        compiler_params=pltpu.CompilerParams(
            dimension_semantics=("arbitrary", "arbitrary"),
            vmem_limit_bytes=V7X_VMEM_LIMIT_BYTES),
        name="ffn",
    )(x, h, h, h, w_up, conv_w, w_d, post_g)


def _rope_tables_t(seq):
    t = np.arange(seq)
    freqs = ROPE_THETA ** (-np.arange(ROT_HALF, dtype=np.float64) / ROT_HALF)

    def axis_tables(pos):
        ang = pos.astype(np.float64)[:, None] * freqs[None, :]
        ang = np.concatenate([ang, ang], axis=-1)
        return np.cos(ang), np.sin(ang)

    cos_r, sin_r = axis_tables(t // GRID_W)
    cos_c, sin_c = axis_tables(t % GRID_W)
    sign = np.where(np.arange(AXIS_ROPE_DIM) < ROT_HALF, -1.0, 1.0)
    cos = np.concatenate([cos_r, cos_c], axis=-1).astype(np.float32)
    sin = np.concatenate([sin_r * sign, sin_c * sign], axis=-1).astype(np.float32)

    def tiled(table):
        return np.ascontiguousarray(
            table.reshape(seq // Q_ROWS, Q_ROWS, HEAD_DIM).transpose(0, 2, 1))

    return tiled(cos), tiled(sin)


def kernel(x, mix_pre_g, w_in, gate_b, q_norm_g, k_norm_g, mix_conv_w, w_attn_proj, w_conv_proj,
           w_out, mix_post_g, ffn_pre_g, w_up, ffn_conv_w, w_down, ffn_post_g):
    depth = w_in.shape[0]
    seq, d = x.shape[1], x.shape[2]
    q_w = N_HEADS * HEAD_DIM
    kv_w = N_KV_HEADS * HEAD_DIM
    d_conv = mix_conv_w.shape[2]
    cos_t, sin_t = _rope_tables_t(seq)
    o0 = q_w + 2 * kv_w
    assert d_conv == d and w_in.shape[2] == o0 + 5 * d
    for l in range(depth):
        w_qkv_t = w_in[l][:, :o0].T.astype(BF16)
        qg_t = jnp.broadcast_to(q_norm_g[l][:, None], (HEAD_DIM, Q_ROWS))
        kg_t = jnp.broadcast_to(k_norm_g[l][:, None], (HEAD_DIM, Q_ROWS))
        score_bound = (HEAD_DIM * SCALE * LOG2_E * 1.01
                       * jnp.max(jnp.abs(q_norm_g[l])) * jnp.max(jnp.abs(k_norm_g[l])))
        shift = jnp.logical_not(score_bound <= EXP2_SAFE_SCORE).astype(jnp.int32).reshape(1)
        later_weights = [(w_in[l], 5 * d), (w_attn_proj[l], d), (w_conv_proj[l], d), (w_out[l], d),
                         (w_up[l], w_up.shape[2]), (w_down[l], d)]
        attn, w_in5, w_attn16, w_conv16, w_out16, w_up16, w_down16 = _attention(
            shift, x, mix_pre_g[l][None], w_qkv_t, qg_t, kg_t, cos_t, sin_t, later_weights)
        x, h_ffn = _mixer(x, attn, mix_pre_g[l][None], w_in5, gate_b[l][None], mix_conv_w[l],
                          w_attn16, w_conv16, w_out16, mix_post_g[l][None], ffn_pre_g[l][None])
        x = _ffn(x, h_ffn, w_up16, ffn_conv_w[l], w_down16, ffn_post_g[l][None])
    return x
```

```python
import functools

import numpy as np

import jax
import jax.numpy as jnp
from jax import lax
from jax.experimental import pallas as pl
from jax.experimental.pallas import tpu as pltpu

N_HEADS = 8
N_KV_HEADS = 2
HEAD_DIM = 128
GROUP = N_HEADS // N_KV_HEADS
GRID_W = 64
ROPE_THETA = 10000.0
AXIS_ROPE_DIM = HEAD_DIM // 2
ROT_HALF = AXIS_ROPE_DIM // 2
EPS = 1e-6
SCALE = HEAD_DIM ** -0.5
LOG2_E = 1.4426950408889634

SUBLANES = 8
HALO = 16
ONES_ROWS = 16
V7X_VMEM_LIMIT_BYTES = 56 * 1024 * 1024

ROW_TILE = 1024
Q_ROWS = 512
PROJ_ROWS = 256
ITEMS_PER_STEP = 8
EXP2_SAFE_SCORE = 64.0
FF_CHUNK = 256

F32 = jnp.float32
BF16 = jnp.bfloat16


def _rms_scale(v):
    return lax.rsqrt(jnp.mean(v * v, axis=-1, keepdims=True) + EPS)


def _sigmoid(v):
    return 1.0 / (1.0 + jnp.exp(-v))


def _gelu_tanh(v):
    c = 0.7978845608028654
    return 0.5 * v * (1.0 + jnp.tanh(c * (v + 0.044715 * (v * v * v))))


def _attention_kernel(n_cast, shift_ref, x_hbm, g_ref, wt_ref, qg_t_ref, kg_t_ref, cos_t_ref,
                      sin_t_ref, *rest):
    w_src, rest = rest[:n_cast], rest[n_cast:]
    o_ref, rest = rest[0], rest[1:]
    w_dst, rest = rest[:n_cast], rest[n_cast:]
    h_hbm, rest = rest[0], rest[1:]
    qt_scr, k_scr, vt_scr, s_even, s_odd, m_even, m_odd, x_buf, x_sem, h_all, h_sem = rest[:11]
    stage_f32, stage_bf16, w_sem = rest[11:11 + n_cast], rest[11 + n_cast:11 + 2 * n_cast], rest[-1]
    seq = x_hbm.shape[1]
    n_tiles = seq // Q_ROWS
    batch = pl.program_id(0)

    def cast_copies(k):
        rows, cols = stage_f32[k].shape
        span = pl.ds(pl.multiple_of(batch * rows, rows), rows)
        first_col = w_src[k].shape[1] - cols
        return (pltpu.make_async_copy(w_src[k].at[span, pl.ds(first_col, cols)], stage_f32[k],
                                      w_sem.at[0, k]),
                pltpu.make_async_copy(stage_bf16[k], w_dst[k].at[span, :], w_sem.at[1, k]))

    for k in range(n_cast):
        cast_copies(k)[0].start()
    n_items = N_HEADS * n_tiles
    tile_bits = n_tiles.bit_length() - 1
    group_bits = GROUP.bit_length() - 1
    q_w = N_HEADS * HEAD_DIM
    kv_w = N_KV_HEADS * HEAD_DIM

    def rot_half(v):
        r = ROT_HALF
        return jnp.concatenate([v[r:2 * r], v[0:r], v[3 * r:4 * r], v[2 * r:3 * r]], axis=0)

    def rope_cols(zt, cos_g, sin_g, out_scale):
        inv = lax.rsqrt(jnp.mean(zt * zt, axis=0, keepdims=True) + EPS) * out_scale
        zn = zt * inv
        return zn * cos_g + rot_half(zn) * sin_g

    def x_copy(bi):
        return pltpu.make_async_copy(x_hbm.at[bi], x_buf, x_sem.at[0])

    def proj_step(r):
        rows = pl.ds(r * Q_ROWS, Q_ROWS)
        xt = x_buf[rows, :]
        h = (xt * _rms_scale(xt) * g_ref[...]).astype(BF16)
        h_all[rows, :] = h
        z_t = jnp.concatenate(
            [lax.dot_general(wt_ref[g * PROJ_ROWS:(g + 1) * PROJ_ROWS, :], h,
                             (((1,), (1,)), ((), ())), preferred_element_type=F32)
             for g in range(wt_ref.shape[0] // PROJ_ROWS)], axis=0)
        cos_t = cos_t_ref[r]
        sin_t = sin_t_ref[r]
        cos_q, sin_q = qg_t_ref[...] * cos_t, rot_half(qg_t_ref[...]) * sin_t
        cos_k, sin_k = kg_t_ref[...] * cos_t, rot_half(kg_t_ref[...]) * sin_t
        for hh in range(N_HEADS):
            zq_t = z_t[hh * HEAD_DIM:(hh + 1) * HEAD_DIM]
            qt_scr[hh, r] = rope_cols(zq_t, cos_q, sin_q, SCALE * LOG2_E).astype(BF16)
        for kh in range(N_KV_HEADS):
            zk_t = z_t[q_w + kh * HEAD_DIM:q_w + (kh + 1) * HEAD_DIM]
            k_scr[kh, rows, :] = rope_cols(zk_t, cos_k, sin_k, 1.0).T.astype(BF16)
            zv_t = z_t[q_w + kv_w + kh * HEAD_DIM:q_w + kv_w + (kh + 1) * HEAD_DIM]
            vt_scr[kh, r, 0:HEAD_DIM, :] = zv_t.astype(BF16)
            vt_scr[kh, r, HEAD_DIM:, :] = jnp.ones((ONES_ROWS, Q_ROWS), BF16)

    @pl.when(batch == 0)
    def _():
        x_copy(0).start()

    x_copy(batch).wait()
    for r in range(n_tiles):
        proj_step(r)

    for k in range(n_cast):
        load, store = cast_copies(k)
        load.wait()
        stage_bf16[k][...] = stage_f32[k][...].astype(BF16)
        store.start()
    h_copy = pltpu.make_async_copy(h_all, h_hbm.at[batch], h_sem.at[0])
    h_copy.start()

    def split(item):
        hh = item >> tile_bits
        t = item & (n_tiles - 1)
        kh = hh >> group_bits
        return hh, t, kh

    def tile_rows(t):
        start = t * Q_ROWS
        if not isinstance(start, int):
            start = pl.multiple_of(start, Q_ROWS)
        return pl.ds(start, Q_ROWS)

    def scores(item, s_scr, m_scr):
        hh, t, kh = split(item)
        s_t = jnp.dot(k_scr[kh], qt_scr[hh, t], preferred_element_type=F32)
        s_scr[...] = s_t
        m_scr[...] = jnp.max(s_t, axis=0, keepdims=True)

    def finish(item, s_scr, m_scr):
        attend(item, jnp.exp2(s_scr[...] - m_scr[...]).astype(BF16))

    def attend(item, p16, denom=None):
        hh, t, kh = split(item)
        v_rows = HEAD_DIM + (ONES_ROWS if denom is None else 0)
        acc = jnp.dot(vt_scr[kh, 0, 0:v_rows, :], p16[0:Q_ROWS], preferred_element_type=F32)
        for kt in range(1, n_tiles):
            acc = acc + jnp.dot(vt_scr[kh, kt, 0:v_rows, :], p16[kt * Q_ROWS:(kt + 1) * Q_ROWS],
                                preferred_element_type=F32)
        if denom is None:
            denom = acc[HEAD_DIM:HEAD_DIM + 1]
        o_ref[0, hh, t] = (acc[0:HEAD_DIM] * (1.0 / denom)).astype(BF16)

    def pair_step(j, carry):
        scores(2 * j + 1, s_odd, m_odd)
        finish(2 * j, s_even, m_even)
        scores(2 * j + 2, s_even, m_even)
        finish(2 * j + 1, s_odd, m_odd)
        return carry

    def unshifted_step(j, carry):
        for i in range(ITEMS_PER_STEP):
            item = ITEMS_PER_STEP * j + i
            hh, t, kh = split(item)
            p_t = jnp.exp2(jnp.dot(k_scr[kh], qt_scr[hh, t], preferred_element_type=F32))
            attend(item, p_t.astype(BF16), jnp.sum(p_t, axis=0, keepdims=True))
        return carry

    @pl.when(batch + 1 < pl.num_programs(0))
    def _():
        x_copy(batch + 1).start()

    shift = shift_ref[0] != 0

    @pl.when(shift)
    def _():
        scores(0, s_even, m_even)
        lax.fori_loop(0, n_items // 2 - 1, pair_step, 0)
        scores(n_items - 1, s_odd, m_odd)
        finish(n_items - 2, s_even, m_even)
        finish(n_items - 1, s_odd, m_odd)

    @pl.when(jnp.logical_not(shift))
    def _():
        lax.fori_loop(0, n_items // ITEMS_PER_STEP, unshifted_step, 0)

    for k in range(n_cast):
        cast_copies(k)[1].wait()
    h_copy.wait()


def _attention(shift, x, pre_g, w_qkv_t, qg_t, kg_t, cos_t, sin_t, cast_weights):
    b, s, d = x.shape
    n_tiles = s // Q_ROWS
    const2 = lambda i: (0, 0)
    const3 = lambda i: (0, 0, 0)
    once = pl.Buffered(1)
    hbm = pl.BlockSpec(memory_space=pl.ANY)
    n_cast = len(cast_weights)
    slabs = []
    for w, n_cols in cast_weights:
        rows, rem = divmod(w.shape[0], b)
        assert rem == 0 and rows % HALO == 0
        slabs.append((rows, n_cols))
    return pl.pallas_call(
        functools.partial(_attention_kernel, n_cast),
        out_shape=[jax.ShapeDtypeStruct((b, N_HEADS, n_tiles, HEAD_DIM, Q_ROWS), BF16)]
        + [jax.ShapeDtypeStruct((w.shape[0], n_cols), BF16) for w, n_cols in cast_weights]
        + [jax.ShapeDtypeStruct((b, s, d), BF16)],
        grid=(b,),
        in_specs=[
            pl.BlockSpec(memory_space=pltpu.SMEM),
            pl.BlockSpec(memory_space=pl.ANY),
            pl.BlockSpec((1, d), const2),
            pl.BlockSpec(w_qkv_t.shape, const2, pipeline_mode=once),
            pl.BlockSpec(qg_t.shape, const2),
            pl.BlockSpec(kg_t.shape, const2),
            pl.BlockSpec(cos_t.shape, const3, pipeline_mode=once),
            pl.BlockSpec(sin_t.shape, const3, pipeline_mode=once),
        ] + [hbm] * n_cast,
        out_specs=[pl.BlockSpec((1, N_HEADS, n_tiles, HEAD_DIM, Q_ROWS),
                                lambda i: (i, 0, 0, 0, 0))] + [hbm] * (n_cast + 1),
        scratch_shapes=[
            pltpu.VMEM((N_HEADS, n_tiles, HEAD_DIM, Q_ROWS), BF16),
            pltpu.VMEM((N_KV_HEADS, s, HEAD_DIM), BF16),
            pltpu.VMEM((N_KV_HEADS, n_tiles, HEAD_DIM + ONES_ROWS, Q_ROWS), BF16),
            pltpu.VMEM((s, Q_ROWS), F32),
            pltpu.VMEM((s, Q_ROWS), F32),
            pltpu.VMEM((1, Q_ROWS), F32),
            pltpu.VMEM((1, Q_ROWS), F32),
            pltpu.VMEM((s, d), F32),
            pltpu.SemaphoreType.DMA((1,)),
            pltpu.VMEM((s, d), BF16),
            pltpu.SemaphoreType.DMA((1,)),
        ] + [pltpu.VMEM(slab, F32) for slab in slabs] + [pltpu.VMEM(slab, BF16) for slab in slabs]
        + [pltpu.SemaphoreType.DMA((2, n_cast))],
        compiler_params=pltpu.CompilerParams(
            dimension_semantics=("arbitrary",), vmem_limit_bytes=V7X_VMEM_LIMIT_BYTES),
        name="attention",
    )(shift, x, pre_g, w_qkv_t, qg_t, kg_t, cos_t, sin_t, *[w for w, _ in cast_weights])


def _halo_specs(seq, d):
    per_tile = ROW_TILE // HALO
    last = seq // HALO - 1
    main = pl.BlockSpec((1, ROW_TILE, d), lambda b, i: (b, i, 0))
    prev = pl.BlockSpec((1, HALO, d), lambda b, i: (b, jnp.maximum(i * per_tile - 1, 0), 0))
    nxt = pl.BlockSpec((1, HALO, d), lambda b, i: (b, jnp.minimum((i + 1) * per_tile, last), 0))
    return main, prev, nxt


def _normed_with_halo(h_scr, x_ref, xp_ref, xn_ref, g_ref):
    i = pl.program_id(1)
    gain = g_ref[...]

    def normed(v):
        return v * _rms_scale(v) * gain

    keep_prev = (i > 0).astype(F32)
    keep_next = (i < pl.num_programs(1) - 1).astype(F32)
    h_scr[0:HALO, :] = (normed(xp_ref[0]) * keep_prev).astype(BF16)
    h_scr[HALO:HALO + ROW_TILE, :] = normed(x_ref[0]).astype(BF16)
    h_scr[HALO + ROW_TILE:, :] = (normed(xn_ref[0]) * keep_next).astype(BF16)


def _conv3(ext, w):
    n = ROW_TILE // SUBLANES
    c = ext.shape[1]
    win = ext[HALO - SUBLANES:HALO + ROW_TILE + SUBLANES].reshape(n + 2, SUBLANES, c)
    sub = lax.broadcasted_iota(jnp.int32, (n, SUBLANES, c), 1)
    down = pltpu.roll(win, 1, 1)
    up = pltpu.roll(win, SUBLANES - 1, 1)
    prev = jnp.where(sub == 0, down[0:n], down[1:n + 1])
    nxt = jnp.where(sub == SUBLANES - 1, up[2:n + 2], up[1:n + 1])
    out = prev * w[0:1, :][None] + win[1:n + 1] * w[1:2, :][None] + nxt * w[2:3, :][None]
    return out.reshape(ROW_TILE, c)


def _mixer_kernel(x_ref, h_ref, hp_ref, hn_ref, o_ref, w_in_ref, gate_b_ref, conv_w_ref,
                  w_attn_ref, w_conv_ref, w_out_ref, post_g_ref, out_ref, h_scr):
    d = x_ref.shape[2]
    w_u, w_b, w_c, w_ga, w_gb = (w_in_ref.at[:, pl.ds(k * d, d)] for k in range(5))
    i = pl.program_id(1)
    zeros = jnp.zeros((HALO, d), BF16)
    h_scr[0:HALO, :] = jnp.where(i > 0, hp_ref[0], zeros)
    h_scr[HALO:HALO + ROW_TILE, :] = h_ref[0]
    h_scr[HALO + ROW_TILE:, :] = jnp.where(i < pl.num_programs(1) - 1, hn_ref[0], zeros)
    h_ext = h_scr[...]
    u = jnp.dot(h_ext, w_u[...], preferred_element_type=F32)
    c = jnp.dot(h_ext, w_c[...], preferred_element_type=F32)
    h = h_scr[pl.ds(HALO, ROW_TILE), :]
    b_gate = jnp.dot(h, w_b[...], preferred_element_type=F32)
    conv = (b_gate * _conv3(c * u, conv_w_ref[...])).astype(BF16)
    y_b = jnp.dot(conv, w_conv_ref[...], preferred_element_type=F32)
    attn_t = jnp.concatenate(
        [jnp.concatenate([o_ref[0, hh, tt] for tt in range(ROW_TILE // Q_ROWS)], axis=1)
         for hh in range(N_HEADS)], axis=0)
    y_a = lax.dot_general(attn_t, w_attn_ref[...], (((0,), (0,)), ((), ())),
                          preferred_element_type=F32)
    g_a = _sigmoid(jnp.dot(h, w_ga[...], preferred_element_type=F32) + gate_b_ref[:, :d])
    g_b = _sigmoid(jnp.dot(h, w_gb[...], preferred_element_type=F32) + gate_b_ref[:, d:])
    merged = (g_a * y_a + g_b * y_b).astype(BF16)
    out = jnp.dot(merged, w_out_ref[...], preferred_element_type=F32)
    out_ref[0] = x_ref[0] + out * _rms_scale(out) * post_g_ref[...]


def _mixer(x, h, o, w_in5, gate_b, conv_w, w_attn, w_conv, w_out, post_g):
    b, s, d = x.shape
    main, prev, nxt = _halo_specs(s, d)
    const = lambda bi, i: (0, 0)
    full = lambda a: pl.BlockSpec(a.shape, const)
    return pl.pallas_call(
        _mixer_kernel,
        out_shape=jax.ShapeDtypeStruct((b, s, d), F32),
        grid=(b, s // ROW_TILE),
        in_specs=[main, main, prev, nxt,
                  pl.BlockSpec((1, N_HEADS, ROW_TILE // Q_ROWS, HEAD_DIM, Q_ROWS),
                               lambda bi, i: (bi, 0, i, 0, 0)),
                  full(w_in5), full(gate_b), full(conv_w), full(w_attn),
                  full(w_conv), full(w_out), full(post_g)],
        out_specs=main,
        scratch_shapes=[pltpu.VMEM((ROW_TILE + 2 * HALO, d), BF16)],
        compiler_params=pltpu.CompilerParams(
            dimension_semantics=("arbitrary", "arbitrary"),
            vmem_limit_bytes=V7X_VMEM_LIMIT_BYTES),
        name="mixer",
    )(x, h, h, h, o, w_in5, gate_b, conv_w, w_attn, w_conv, w_out, post_g)


def _ffn_kernel(x_ref, xp_ref, xn_ref, pre_g_ref, w_up_ref, conv_w_ref, w_d_ref, post_g_ref,
                out_ref, h_scr, hid_scr):
    d_ff = conv_w_ref.shape[1]
    _normed_with_halo(h_scr, x_ref, xp_ref, xn_ref, pre_g_ref)
    h_ext = h_scr[...]
    h = h_scr[pl.ds(HALO, ROW_TILE), :]
    for j in range(d_ff // FF_CHUNK):
        lo, hi = j * FF_CHUNK, (j + 1) * FF_CHUNK
        a_ext = jnp.dot(h_ext, w_up_ref[:, lo:hi], preferred_element_type=F32)
        val = jnp.dot(h, w_up_ref[:, d_ff + lo:d_ff + hi], preferred_element_type=F32)
        gate = _gelu_tanh(_conv3(a_ext, conv_w_ref[:, lo:hi]))
        hid_scr[:, lo:hi] = (gate * val).astype(BF16)
    out = jnp.dot(hid_scr[...], w_d_ref[...], preferred_element_type=F32)
    out_ref[0] = x_ref[0] + out * _rms_scale(out) * post_g_ref[...]


def _ffn(x, pre_g, w_up, conv_w, w_d, post_g):
    b, s, d = x.shape
    main, prev, nxt = _halo_specs(s, d)
    const2 = lambda bi, i: (0, 0)
    once = pl.Buffered(1)
    return pl.pallas_call(
        _ffn_kernel,
        out_shape=jax.ShapeDtypeStruct((b, s, d), F32),
        grid=(b, s // ROW_TILE),
        in_specs=[main, prev, nxt,
                  pl.BlockSpec(pre_g.shape, const2),
                  pl.BlockSpec(w_up.shape, const2, pipeline_mode=once),
                  pl.BlockSpec(conv_w.shape, const2),
                  pl.BlockSpec(w_d.shape, const2, pipeline_mode=once),
                  pl.BlockSpec(post_g.shape, const2)],
        out_specs=pl.BlockSpec((1, ROW_TILE, d), lambda bi, i: (bi, i, 0)),
        scratch_shapes=[
            pltpu.VMEM((ROW_TILE + 2 * HALO, d), BF16),
            pltpu.VMEM((ROW_TILE, w_d.shape[0]), BF16),
        ],
        compiler_params=pltpu.CompilerParams(
            dimension_semantics=("arbitrary", "arbitrary"),
            vmem_limit_bytes=V7X_VMEM_LIMIT_BYTES),
        name="ffn",
    )(x, x, x, pre_g, w_up, conv_w, w_d, post_g)


def _rope_tables_t(seq):
    t = np.arange(seq)
    freqs = ROPE_THETA ** (-np.arange(ROT_HALF, dtype=np.float64) / ROT_HALF)

    def axis_tables(pos):
        ang = pos.astype(np.float64)[:, None] * freqs[None, :]
        ang = np.concatenate([ang, ang], axis=-1)
        return np.cos(ang), np.sin(ang)

    cos_r, sin_r = axis_tables(t // GRID_W)
    cos_c, sin_c = axis_tables(t % GRID_W)
    sign = np.where(np.arange(AXIS_ROPE_DIM) < ROT_HALF, -1.0, 1.0)
    cos = np.concatenate([cos_r, cos_c], axis=-1).astype(np.float32)
    sin = np.concatenate([sin_r * sign, sin_c * sign], axis=-1).astype(np.float32)

    def tiled(table):
        return np.ascontiguousarray(
            table.reshape(seq // Q_ROWS, Q_ROWS, HEAD_DIM).transpose(0, 2, 1))

    return tiled(cos), tiled(sin)


def kernel(x, mix_pre_g, w_in, gate_b, q_norm_g, k_norm_g, mix_conv_w, w_attn_proj, w_conv_proj,
           w_out, mix_post_g, ffn_pre_g, w_up, ffn_conv_w, w_down, ffn_post_g):
    depth = w_in.shape[0]
    seq, d = x.shape[1], x.shape[2]
    q_w = N_HEADS * HEAD_DIM
    kv_w = N_KV_HEADS * HEAD_DIM
    d_conv = mix_conv_w.shape[2]
    cos_t, sin_t = _rope_tables_t(seq)
    o0 = q_w + 2 * kv_w
    assert d_conv == d and w_in.shape[2] == o0 + 5 * d
    for l in range(depth):
        w_qkv_t = w_in[l][:, :o0].T.astype(BF16)
        qg_t = jnp.broadcast_to(q_norm_g[l][:, None], (HEAD_DIM, Q_ROWS))
        kg_t = jnp.broadcast_to(k_norm_g[l][:, None], (HEAD_DIM, Q_ROWS))
        score_bound = (HEAD_DIM * SCALE * LOG2_E * 1.01
                       * jnp.max(jnp.abs(q_norm_g[l])) * jnp.max(jnp.abs(k_norm_g[l])))
        shift = jnp.logical_not(score_bound <= EXP2_SAFE_SCORE).astype(jnp.int32).reshape(1)
        later_weights = [(w_in[l], 5 * d), (w_attn_proj[l], d), (w_conv_proj[l], d), (w_out[l], d),
                         (w_up[l], w_up.shape[2]), (w_down[l], d)]
        attn, w_in5, w_attn16, w_conv16, w_out16, w_up16, w_down16, h_mix = _attention(
            shift, x, mix_pre_g[l][None], w_qkv_t, qg_t, kg_t, cos_t, sin_t, later_weights)
        x = _mixer(x, h_mix, attn, w_in5, gate_b[l][None], mix_conv_w[l],
                   w_attn16, w_conv16, w_out16, mix_post_g[l][None])
        x = _ffn(x, ffn_pre_g[l][None], w_up16, ffn_conv_w[l], w_down16, ffn_post_g[l][None])
    return x
```

```python
import functools

import numpy as np

import jax
import jax.numpy as jnp
from jax import lax
from jax.experimental import pallas as pl
from jax.experimental.pallas import tpu as pltpu

N_HEADS = 8
N_KV_HEADS = 2
HEAD_DIM = 128
GROUP = N_HEADS // N_KV_HEADS
GRID_W = 64
ROPE_THETA = 10000.0
AXIS_ROPE_DIM = HEAD_DIM // 2
ROT_HALF = AXIS_ROPE_DIM // 2
EPS = 1e-6
SCALE = HEAD_DIM ** -0.5
LOG2_E = 1.4426950408889634

SUBLANES = 8
HALO = 16
ONES_ROWS = 16
V7X_VMEM_LIMIT_BYTES = 56 * 1024 * 1024

ROW_TILE = 1024
Q_ROWS = 512
PROJ_ROWS = 256
ITEMS_PER_STEP = 8
EXP2_SAFE_SCORE = 64.0
FF_CHUNK = 256

F32 = jnp.float32
BF16 = jnp.bfloat16


def _rms_scale(v):
    return lax.rsqrt(jnp.mean(v * v, axis=-1, keepdims=True) + EPS)


def _sigmoid(v):
    return 1.0 / (1.0 + jnp.exp(-v))


def _gelu_tanh(v):
    c = 0.7978845608028654
    return 0.5 * v * (1.0 + jnp.tanh(c * (v + 0.044715 * (v * v * v))))


def _attention_kernel(n_cast, shift_ref, x_hbm, g_ref, wt_ref, qg_t_ref, kg_t_ref, cos_t_ref,
                      sin_t_ref, *rest):
    w_src, rest = rest[:n_cast], rest[n_cast:]
    o_ref, rest = rest[0], rest[1:]
    w_dst, rest = rest[:n_cast], rest[n_cast:]
    h_hbm, rest = rest[0], rest[1:]
    qt_scr, k_scr, vt_scr, s_even, s_odd, m_even, m_odd, x_buf, x_sem, h_all, h_sem = rest[:11]
    stage_f32, stage_bf16, w_sem = rest[11:11 + n_cast], rest[11 + n_cast:11 + 2 * n_cast], rest[-1]
    seq = x_hbm.shape[1]
    n_tiles = seq // Q_ROWS
    batch = pl.program_id(0)

    def cast_copies(k):
        rows, cols = stage_f32[k].shape
        span = pl.ds(pl.multiple_of(batch * rows, rows), rows)
        first_col = w_src[k].shape[1] - cols
        return (pltpu.make_async_copy(w_src[k].at[span, pl.ds(first_col, cols)], stage_f32[k],
                                      w_sem.at[0, k]),
                pltpu.make_async_copy(stage_bf16[k], w_dst[k].at[span, :], w_sem.at[1, k]))

    for k in range(n_cast):
        cast_copies(k)[0].start(priority=k % 2)
    n_items = N_HEADS * n_tiles
    tile_bits = n_tiles.bit_length() - 1
    group_bits = GROUP.bit_length() - 1
    q_w = N_HEADS * HEAD_DIM
    kv_w = N_KV_HEADS * HEAD_DIM

    def rot_half(v):
        r = ROT_HALF
        return jnp.concatenate([v[r:2 * r], v[0:r], v[3 * r:4 * r], v[2 * r:3 * r]], axis=0)

    def rope_cols(zt, cos_g, sin_g, out_scale):
        inv = lax.rsqrt(jnp.mean(zt * zt, axis=0, keepdims=True) + EPS) * out_scale
        zn = zt * inv
        return zn * cos_g + rot_half(zn) * sin_g

    def x_copy(bi):
        return pltpu.make_async_copy(x_hbm.at[bi], x_buf, x_sem.at[0])

    def proj_step(r):
        rows = pl.ds(r * Q_ROWS, Q_ROWS)
        xt = x_buf[rows, :]
        h = (xt * _rms_scale(xt) * g_ref[...]).astype(BF16)
        h_all[rows, :] = h
        z_t = jnp.concatenate(
            [lax.dot_general(wt_ref[g * PROJ_ROWS:(g + 1) * PROJ_ROWS, :], h,
                             (((1,), (1,)), ((), ())), preferred_element_type=F32)
             for g in range(wt_ref.shape[0] // PROJ_ROWS)], axis=0)
        cos_t = cos_t_ref[r]
        sin_t = sin_t_ref[r]
        cos_q, sin_q = qg_t_ref[...] * cos_t, rot_half(qg_t_ref[...]) * sin_t
        cos_k, sin_k = kg_t_ref[...] * cos_t, rot_half(kg_t_ref[...]) * sin_t
        for hh in range(N_HEADS):
            zq_t = z_t[hh * HEAD_DIM:(hh + 1) * HEAD_DIM]
            qt_scr[hh, r] = rope_cols(zq_t, cos_q, sin_q, SCALE * LOG2_E).astype(BF16)
        for kh in range(N_KV_HEADS):
            zk_t = z_t[q_w + kh * HEAD_DIM:q_w + (kh + 1) * HEAD_DIM]
            k_scr[kh, rows, :] = rope_cols(zk_t, cos_k, sin_k, 1.0).T.astype(BF16)
            zv_t = z_t[q_w + kv_w + kh * HEAD_DIM:q_w + kv_w + (kh + 1) * HEAD_DIM]
            vt_scr[kh, r, 0:HEAD_DIM, :] = zv_t.astype(BF16)
            vt_scr[kh, r, HEAD_DIM:, :] = jnp.ones((ONES_ROWS, Q_ROWS), BF16)

    @pl.when(batch == 0)
    def _():
        x_copy(0).start()

    x_copy(batch).wait()
    for r in range(n_tiles):
        proj_step(r)

    for k in range(n_cast):
        load, store = cast_copies(k)
        load.wait()
        stage_bf16[k][...] = stage_f32[k][...].astype(BF16)
        store.start(priority=k % 2)
    h_copy = pltpu.make_async_copy(h_all, h_hbm.at[batch], h_sem.at[0])
    h_copy.start()

    def split(item):
        hh = item >> tile_bits
        t = item & (n_tiles - 1)
        kh = hh >> group_bits
        return hh, t, kh

    def tile_rows(t):
        start = t * Q_ROWS
        if not isinstance(start, int):
            start = pl.multiple_of(start, Q_ROWS)
        return pl.ds(start, Q_ROWS)

    def scores(item, s_scr, m_scr):
        hh, t, kh = split(item)
        s_t = jnp.dot(k_scr[kh], qt_scr[hh, t], preferred_element_type=F32)
        s_scr[...] = s_t
        m_scr[...] = jnp.max(s_t, axis=0, keepdims=True)

    def finish(item, s_scr, m_scr):
        attend(item, jnp.exp2(s_scr[...] - m_scr[...]).astype(BF16))

    def attend(item, p16, denom=None):
        hh, t, kh = split(item)
        v_rows = HEAD_DIM + (ONES_ROWS if denom is None else 0)
        acc = jnp.dot(vt_scr[kh, 0, 0:v_rows, :], p16[0:Q_ROWS], preferred_element_type=F32)
        for kt in range(1, n_tiles):
            acc = acc + jnp.dot(vt_scr[kh, kt, 0:v_rows, :], p16[kt * Q_ROWS:(kt + 1) * Q_ROWS],
                                preferred_element_type=F32)
        if denom is None:
            denom = acc[HEAD_DIM:HEAD_DIM + 1]
        o_ref[0, hh, t] = (acc[0:HEAD_DIM] * (1.0 / denom)).astype(BF16)

    def pair_step(j, carry):
        scores(2 * j + 1, s_odd, m_odd)
        finish(2 * j, s_even, m_even)
        scores(2 * j + 2, s_even, m_even)
        finish(2 * j + 1, s_odd, m_odd)
        return carry

    def unshifted_step(j, carry):
        for i in range(ITEMS_PER_STEP):
            item = ITEMS_PER_STEP * j + i
            hh, t, kh = split(item)
            p_t = jnp.exp2(jnp.dot(k_scr[kh], qt_scr[hh, t], preferred_element_type=F32))
            attend(item, p_t.astype(BF16), jnp.sum(p_t, axis=0, keepdims=True))
        return carry

    @pl.when(batch + 1 < pl.num_programs(0))
    def _():
        x_copy(batch + 1).start()

    shift = shift_ref[0] != 0

    @pl.when(shift)
    def _():
        scores(0, s_even, m_even)
        lax.fori_loop(0, n_items // 2 - 1, pair_step, 0)
        scores(n_items - 1, s_odd, m_odd)
        finish(n_items - 2, s_even, m_even)
        finish(n_items - 1, s_odd, m_odd)

    @pl.when(jnp.logical_not(shift))
    def _():
        lax.fori_loop(0, n_items // ITEMS_PER_STEP, unshifted_step, 0)

    for k in range(n_cast):
        cast_copies(k)[1].wait()
    h_copy.wait()


def _attention(shift, x, pre_g, w_qkv_t, qg_t, kg_t, cos_t, sin_t, cast_weights):
    b, s, d = x.shape
    n_tiles = s // Q_ROWS
    const2 = lambda i: (0, 0)
    const3 = lambda i: (0, 0, 0)
    once = pl.Buffered(1)
    hbm = pl.BlockSpec(memory_space=pl.ANY)
    n_cast = len(cast_weights)
    slabs = []
    for w, n_cols in cast_weights:
        rows, rem = divmod(w.shape[0], b)
        assert rem == 0 and rows % HALO == 0
        slabs.append((rows, n_cols))
    return pl.pallas_call(
        functools.partial(_attention_kernel, n_cast),
        out_shape=[jax.ShapeDtypeStruct((b, N_HEADS, n_tiles, HEAD_DIM, Q_ROWS), BF16)]
        + [jax.ShapeDtypeStruct((w.shape[0], n_cols), BF16) for w, n_cols in cast_weights]
        + [jax.ShapeDtypeStruct((b, s, d), BF16)],
        grid=(b,),
        in_specs=[
            pl.BlockSpec(memory_space=pltpu.SMEM),
            pl.BlockSpec(memory_space=pl.ANY),
            pl.BlockSpec((1, d), const2),
            pl.BlockSpec(w_qkv_t.shape, const2, pipeline_mode=once),
            pl.BlockSpec(qg_t.shape, const2),
            pl.BlockSpec(kg_t.shape, const2),
            pl.BlockSpec(cos_t.shape, const3, pipeline_mode=once),
            pl.BlockSpec(sin_t.shape, const3, pipeline_mode=once),
        ] + [hbm] * n_cast,
        out_specs=[pl.BlockSpec((1, N_HEADS, n_tiles, HEAD_DIM, Q_ROWS),
                                lambda i: (i, 0, 0, 0, 0))] + [hbm] * (n_cast + 1),
        scratch_shapes=[
            pltpu.VMEM((N_HEADS, n_tiles, HEAD_DIM, Q_ROWS), BF16),
            pltpu.VMEM((N_KV_HEADS, s, HEAD_DIM), BF16),
            pltpu.VMEM((N_KV_HEADS, n_tiles, HEAD_DIM + ONES_ROWS, Q_ROWS), BF16),
            pltpu.VMEM((s, Q_ROWS), F32),
            pltpu.VMEM((s, Q_ROWS), F32),
            pltpu.VMEM((1, Q_ROWS), F32),
            pltpu.VMEM((1, Q_ROWS), F32),
            pltpu.VMEM((s, d), F32),
            pltpu.SemaphoreType.DMA((1,)),
            pltpu.VMEM((s, d), BF16),
            pltpu.SemaphoreType.DMA((1,)),
        ] + [pltpu.VMEM(slab, F32) for slab in slabs] + [pltpu.VMEM(slab, BF16) for slab in slabs]
        + [pltpu.SemaphoreType.DMA((2, n_cast))],
        compiler_params=pltpu.CompilerParams(
            dimension_semantics=("arbitrary",), vmem_limit_bytes=V7X_VMEM_LIMIT_BYTES),
        name="attention",
    )(shift, x, pre_g, w_qkv_t, qg_t, kg_t, cos_t, sin_t, *[w for w, _ in cast_weights])


def _halo_specs(seq, d):
    per_tile = ROW_TILE // HALO
    last = seq // HALO - 1
    main = pl.BlockSpec((1, ROW_TILE, d), lambda b, i: (b, i, 0))
    prev = pl.BlockSpec((1, HALO, d), lambda b, i: (b, jnp.maximum(i * per_tile - 1, 0), 0))
    nxt = pl.BlockSpec((1, HALO, d), lambda b, i: (b, jnp.minimum((i + 1) * per_tile, last), 0))
    return main, prev, nxt


def _normed_with_halo(h_scr, x_ref, xp_ref, xn_ref, g_ref):
    i = pl.program_id(1)
    gain = g_ref[...]

    def normed(v):
        return v * _rms_scale(v) * gain

    keep_prev = (i > 0).astype(F32)
    keep_next = (i < pl.num_programs(1) - 1).astype(F32)
    h_scr[0:HALO, :] = (normed(xp_ref[0]) * keep_prev).astype(BF16)
    h_scr[HALO:HALO + ROW_TILE, :] = normed(x_ref[0]).astype(BF16)
    h_scr[HALO + ROW_TILE:, :] = (normed(xn_ref[0]) * keep_next).astype(BF16)


def _conv3(ext, w):
    n = ROW_TILE // SUBLANES
    c = ext.shape[1]
    win = ext[HALO - SUBLANES:HALO + ROW_TILE + SUBLANES].reshape(n + 2, SUBLANES, c)
    sub = lax.broadcasted_iota(jnp.int32, (n, SUBLANES, c), 1)
    down = pltpu.roll(win, 1, 1)
    up = pltpu.roll(win, SUBLANES - 1, 1)
    prev = jnp.where(sub == 0, down[0:n], down[1:n + 1])
    nxt = jnp.where(sub == SUBLANES - 1, up[2:n + 2], up[1:n + 1])
    out = prev * w[0:1, :][None] + win[1:n + 1] * w[1:2, :][None] + nxt * w[2:3, :][None]
    return out.reshape(ROW_TILE, c)


def _mixer_kernel(x_ref, h_ref, hp_ref, hn_ref, o_ref, w_in_ref, gate_b_ref, conv_w_ref,
                  w_attn_ref, w_conv_ref, w_out_ref, post_g_ref, out_ref, h_scr):
    d = x_ref.shape[2]
    w_u, w_b, w_c, w_ga, w_gb = (w_in_ref.at[:, pl.ds(k * d, d)] for k in range(5))
    i = pl.program_id(1)
    zeros = jnp.zeros((HALO, d), BF16)
    h_scr[0:HALO, :] = jnp.where(i > 0, hp_ref[0], zeros)
    h_scr[HALO:HALO + ROW_TILE, :] = h_ref[0]
    h_scr[HALO + ROW_TILE:, :] = jnp.where(i < pl.num_programs(1) - 1, hn_ref[0], zeros)
    h_ext = h_scr[...]
    u = jnp.dot(h_ext, w_u[...], preferred_element_type=F32)
    c = jnp.dot(h_ext, w_c[...], preferred_element_type=F32)
    h = h_scr[pl.ds(HALO, ROW_TILE), :]
    b_gate = jnp.dot(h, w_b[...], preferred_element_type=F32)
    conv = (b_gate * _conv3(c * u, conv_w_ref[...])).astype(BF16)
    y_b = jnp.dot(conv, w_conv_ref[...], preferred_element_type=F32)
    attn_t = jnp.concatenate(
        [jnp.concatenate([o_ref[0, hh, tt] for tt in range(ROW_TILE // Q_ROWS)], axis=1)
         for hh in range(N_HEADS)], axis=0)
    y_a = lax.dot_general(attn_t, w_attn_ref[...], (((0,), (0,)), ((), ())),
                          preferred_element_type=F32)
    g_a = _sigmoid(jnp.dot(h, w_ga[...], preferred_element_type=F32) + gate_b_ref[:, :d])
    g_b = _sigmoid(jnp.dot(h, w_gb[...], preferred_element_type=F32) + gate_b_ref[:, d:])
    merged = (g_a * y_a + g_b * y_b).astype(BF16)
    out = jnp.dot(merged, w_out_ref[...], preferred_element_type=F32)
    out_ref[0] = x_ref[0] + out * _rms_scale(out) * post_g_ref[...]


def _mixer(x, h, o, w_in5, gate_b, conv_w, w_attn, w_conv, w_out, post_g):
    b, s, d = x.shape
    main, prev, nxt = _halo_specs(s, d)
    const = lambda bi, i: (0, 0)
    full = lambda a: pl.BlockSpec(a.shape, const)
    return pl.pallas_call(
        _mixer_kernel,
        out_shape=jax.ShapeDtypeStruct((b, s, d), F32),
        grid=(b, s // ROW_TILE),
        in_specs=[main, main, prev, nxt,
                  pl.BlockSpec((1, N_HEADS, ROW_TILE // Q_ROWS, HEAD_DIM, Q_ROWS),
                               lambda bi, i: (bi, 0, i, 0, 0)),
                  full(w_in5), full(gate_b), full(conv_w), full(w_attn),
                  full(w_conv), full(w_out), full(post_g)],
        out_specs=main,
        scratch_shapes=[pltpu.VMEM((ROW_TILE + 2 * HALO, d), BF16)],
        compiler_params=pltpu.CompilerParams(
            dimension_semantics=("arbitrary", "arbitrary"),
            vmem_limit_bytes=V7X_VMEM_LIMIT_BYTES),
        name="mixer",
    )(x, h, h, h, o, w_in5, gate_b, conv_w, w_attn, w_conv, w_out, post_g)


def _ffn_kernel(x_ref, xp_ref, xn_ref, pre_g_ref, w_up_ref, conv_w_ref, w_d_ref, post_g_ref,
                out_ref, h_scr, hid_scr):
    d_ff = conv_w_ref.shape[1]
    _normed_with_halo(h_scr, x_ref, xp_ref, xn_ref, pre_g_ref)
    h_ext = h_scr[...]
    h = h_scr[pl.ds(HALO, ROW_TILE), :]
    for j in range(d_ff // FF_CHUNK):
        lo, hi = j * FF_CHUNK, (j + 1) * FF_CHUNK
        a_ext = jnp.dot(h_ext, w_up_ref[:, lo:hi], preferred_element_type=F32)
        val = jnp.dot(h, w_up_ref[:, d_ff + lo:d_ff + hi], preferred_element_type=F32)
        gate = _gelu_tanh(_conv3(a_ext, conv_w_ref[:, lo:hi]))
        hid_scr[:, lo:hi] = (gate * val).astype(BF16)
    out = jnp.dot(hid_scr[...], w_d_ref[...], preferred_element_type=F32)
    out_ref[0] = x_ref[0] + out * _rms_scale(out) * post_g_ref[...]


def _ffn(x, pre_g, w_up, conv_w, w_d, post_g):
    b, s, d = x.shape
    main, prev, nxt = _halo_specs(s, d)
    const2 = lambda bi, i: (0, 0)
    once = pl.Buffered(1)
    return pl.pallas_call(
        _ffn_kernel,
        out_shape=jax.ShapeDtypeStruct((b, s, d), F32),
        grid=(b, s // ROW_TILE),
        in_specs=[main, prev, nxt,
                  pl.BlockSpec(pre_g.shape, const2),
                  pl.BlockSpec(w_up.shape, const2, pipeline_mode=once),
                  pl.BlockSpec(conv_w.shape, const2),
                  pl.BlockSpec(w_d.shape, const2, pipeline_mode=once),
                  pl.BlockSpec(post_g.shape, const2)],
        out_specs=pl.BlockSpec((1, ROW_TILE, d), lambda bi, i: (bi, i, 0)),
        scratch_shapes=[
            pltpu.VMEM((ROW_TILE + 2 * HALO, d), BF16),
            pltpu.VMEM((ROW_TILE, w_d.shape[0]), BF16),
        ],
        compiler_params=pltpu.CompilerParams(
            dimension_semantics=("arbitrary", "arbitrary"),
            vmem_limit_bytes=V7X_VMEM_LIMIT_BYTES),
        name="ffn",
    )(x, x, x, pre_g, w_up, conv_w, w_d, post_g)


def _rope_tables_t(seq):
    t = np.arange(seq)
    freqs = ROPE_THETA ** (-np.arange(ROT_HALF, dtype=np.float64) / ROT_HALF)

    def axis_tables(pos):
        ang = pos.astype(np.float64)[:, None] * freqs[None, :]
        ang = np.concatenate([ang, ang], axis=-1)
        return np.cos(ang), np.sin(ang)

    cos_r, sin_r = axis_tables(t // GRID_W)
    cos_c, sin_c = axis_tables(t % GRID_W)
    sign = np.where(np.arange(AXIS_ROPE_DIM) < ROT_HALF, -1.0, 1.0)
    cos = np.concatenate([cos_r, cos_c], axis=-1).astype(np.float32)
    sin = np.concatenate([sin_r * sign, sin_c * sign], axis=-1).astype(np.float32)

    def tiled(table):
        return np.ascontiguousarray(
            table.reshape(seq // Q_ROWS, Q_ROWS, HEAD_DIM).transpose(0, 2, 1))

    return tiled(cos), tiled(sin)


def kernel(x, mix_pre_g, w_in, gate_b, q_norm_g, k_norm_g, mix_conv_w, w_attn_proj, w_conv_proj,
           w_out, mix_post_g, ffn_pre_g, w_up, ffn_conv_w, w_down, ffn_post_g):
    depth = w_in.shape[0]
    seq, d = x.shape[1], x.shape[2]
    q_w = N_HEADS * HEAD_DIM
    kv_w = N_KV_HEADS * HEAD_DIM
    d_conv = mix_conv_w.shape[2]
    cos_t, sin_t = _rope_tables_t(seq)
    o0 = q_w + 2 * kv_w
    assert d_conv == d and w_in.shape[2] == o0 + 5 * d
    for l in range(depth):
        w_qkv_t = w_in[l][:, :o0].T.astype(BF16)
        qg_t = jnp.broadcast_to(q_norm_g[l][:, None], (HEAD_DIM, Q_ROWS))
        kg_t = jnp.broadcast_to(k_norm_g[l][:, None], (HEAD_DIM, Q_ROWS))
        score_bound = (HEAD_DIM * SCALE * LOG2_E * 1.01
                       * jnp.max(jnp.abs(q_norm_g[l])) * jnp.max(jnp.abs(k_norm_g[l])))
        shift = jnp.logical_not(score_bound <= EXP2_SAFE_SCORE).astype(jnp.int32).reshape(1)
        later_weights = [(w_in[l], 5 * d), (w_attn_proj[l], d), (w_conv_proj[l], d), (w_out[l], d),
                         (w_up[l], w_up.shape[2]), (w_down[l], d)]
        attn, w_in5, w_attn16, w_conv16, w_out16, w_up16, w_down16, h_mix = _attention(
            shift, x, mix_pre_g[l][None], w_qkv_t, qg_t, kg_t, cos_t, sin_t, later_weights)
        x = _mixer(x, h_mix, attn, w_in5, gate_b[l][None], mix_conv_w[l],
                   w_attn16, w_conv16, w_out16, mix_post_g[l][None])
        x = _ffn(x, ffn_pre_g[l][None], w_up16, ffn_conv_w[l], w_down16, ffn_post_g[l][None])
    return x
```
